```python
import math
import jax, jax.numpy as jnp
from jax import lax
import numpy as np

D_MODEL = 1024
BATCH = 16
SEQ = 2048
DEPTH = 2

EPS = 1e-6
MIX_W = 512
N_BRANCH = 4
A_HEADS = 4
A_V_DIM = MIX_W // A_HEADS
A_QK_DIM = A_V_DIM // 2
ATTN_Q_BLOCK = 128
B_HEADS = 4
B_HEAD_DIM = MIX_W // B_HEADS
MOBA_BLOCK = 256
MOBA_TOPK = 3
MOBA_Q_CHUNK = 16
C_WIDTH = MIX_W
C_GROUPS = 8
C_GROUP_DIM = C_WIDTH // C_GROUPS
C_CONV = 4
LRU_C = 8.0
D_HEADS = 4
D_HEAD_DIM = MIX_W // D_HEADS
D_CONV = 4
DELTA_CHUNK = 64
IN_SIZES = (MIX_W, MIX_W, MIX_W,
            MIX_W, MIX_W, MIX_W,
            C_WIDTH, C_WIDTH,
            MIX_W, MIX_W, MIX_W, MIX_W,
            D_HEADS, D_HEADS)
IN_COLS = sum(IN_SIZES)
FFN_DENSE = 2816
N_EXPERTS = 8
TOP_K = 2
FFN_EXPERT = 3584
N_DENSE = (DEPTH + 1) // 2
N_MOE = DEPTH // 2

kernel_name = "hybrid_gated_parallel_mixers_moe"


def rmsnorm(x, g, eps=EPS):
    xf = x.astype(jnp.float32)
    y = xf * lax.rsqrt(jnp.mean(xf * xf, axis=-1, keepdims=True) + eps)
    return (y * g.astype(jnp.float32)).astype(x.dtype)


def l2norm(x):
    return x * lax.rsqrt(jnp.sum(x * x, axis=-1, keepdims=True) + 1e-6)


def causal_dwconv(x, w):
    K, C = w.shape
    return lax.conv_general_dilated(x, w[:, None, :].astype(x.dtype), window_strides=(1,),
                                    padding=[(K - 1, 0)], dimension_numbers=('NWC', 'WIO', 'NWC'),
                                    feature_group_count=C)


def split_in_proj(p):
    idx = [int(i) for i in np.cumsum(IN_SIZES)[:-1]]
    return jnp.split(p, idx, axis=-1)


def diff_attention(q, k, v, lam_q1, lam_k1, lam_q2, lam_k2, sub_g, layer_idx):
    B, S, H, _, dq = q.shape
    f32 = jnp.float32
    lam_init = 0.8 - 0.6 * math.exp(-0.3 * layer_idx)
    lam = (jnp.exp(jnp.sum(lam_q1.astype(f32) * lam_k1.astype(f32)))
           - jnp.exp(jnp.sum(lam_q2.astype(f32) * lam_k2.astype(f32))) + lam_init)
    scale = dq ** -0.5
    nqb = S // ATTN_Q_BLOCK
    qb = q.reshape(B, nqb, ATTN_Q_BLOCK, H, 2, dq).swapaxes(0, 1)
    kpos = jnp.arange(S)

    def block(args):
        qi, i = args
        s = jnp.einsum('bqhcd,bkhcd->bhcqk', qi, k).astype(f32) * scale
        qpos = i * ATTN_Q_BLOCK + jnp.arange(ATTN_Q_BLOCK)
        s = jnp.where(kpos[None, :] <= qpos[:, None], s, -jnp.inf)
        p = jax.nn.softmax(s, axis=-1)
        w = (p[:, :, 0] - lam * p[:, :, 1]).astype(v.dtype)
        return jnp.einsum('bhqk,bkhd->bqhd', w, v)

    o = lax.map(block, (qb, jnp.arange(nqb)))
    o = o.swapaxes(0, 1).reshape(B, S, H, -1)
    o = rmsnorm(o, sub_g, eps=1e-5) * (1.0 - lam_init)
    return o.reshape(B, S, -1)


def moba_attention(q, k, v):
    B, S, H, d = q.shape
    f32 = jnp.float32
    nb = -(-S // MOBA_BLOCK)
    pad = nb * MOBA_BLOCK - S
    kp = jnp.pad(k, ((0, 0), (0, pad), (0, 0), (0, 0)))
    vp = jnp.pad(v, ((0, 0), (0, pad), (0, 0), (0, 0)))
    kb = kp.reshape(B, nb, MOBA_BLOCK, H, d).transpose(0, 3, 1, 2, 4)
    vb = vp.reshape(B, nb, MOBA_BLOCK, H, d).transpose(0, 3, 1, 2, 4)
    n_cand = nb - 1
    k_sel = min(MOBA_TOPK, n_cand)
    n_chunk = S // MOBA_Q_CHUNK
    qc = q.reshape(B, n_chunk, MOBA_Q_CHUNK, H, d).swapaxes(0, 1)
    scale = d ** -0.5
    b_idx = jnp.arange(B)[:, None, None, None]
    h_idx = jnp.arange(H)[None, :, None, None]
    kmean = jnp.mean(kb[:, :, :max(n_cand, 1)].astype(f32), axis=3)

    def chunk(args):
        qi, c = args
        start = c * MOBA_Q_CHUNK
        own = start // MOBA_BLOCK
        qpos = start + jnp.arange(MOBA_Q_CHUNK)
        k_own = lax.dynamic_slice_in_dim(kp, own * MOBA_BLOCK, MOBA_BLOCK, axis=1)
        v_own = lax.dynamic_slice_in_dim(vp, own * MOBA_BLOCK, MOBA_BLOCK, axis=1)
        s_own = jnp.einsum('bqhd,blhd->bhql', qi, k_own).astype(f32) * scale
        kpos = own * MOBA_BLOCK + jnp.arange(MOBA_BLOCK)
        s_own = jnp.where(kpos[None, :] <= qpos[:, None], s_own, -jnp.inf)
        if k_sel == 0:
            p = jax.nn.softmax(s_own, axis=-1).astype(v.dtype)
            return jnp.einsum('bhql,blhd->bqhd', p, v_own)
        gate = jnp.einsum('bqhd,bhnd->bhqn', qi.astype(f32), kmean)
        gate = jnp.where(jnp.arange(n_cand) < own, gate, -jnp.inf)
        _, sel = lax.top_k(gate, k_sel)
        ksel = kb[b_idx, h_idx, sel]
        vsel = vb[b_idx, h_idx, sel]
        s_sel = jnp.einsum('bqhd,bhqkld->bhqkl', qi, ksel).astype(f32) * scale
        valid = jnp.arange(k_sel) < own
        s_sel = jnp.where(valid[:, None], s_sel, -jnp.inf).reshape(B, H, MOBA_Q_CHUNK, k_sel * MOBA_BLOCK)
        p = jax.nn.softmax(jnp.concatenate([s_sel, s_own], axis=-1), axis=-1).astype(v.dtype)
        p_sel = p[..., :k_sel * MOBA_BLOCK].reshape(B, H, MOBA_Q_CHUNK, k_sel, MOBA_BLOCK)
        p_own = p[..., k_sel * MOBA_BLOCK:]
        return (jnp.einsum('bhqkl,bhqkld->bqhd', p_sel, vsel)
                + jnp.einsum('bhql,blhd->bqhd', p_own, v_own))

    o = lax.map(chunk, (qc, jnp.arange(n_chunk)))
    return o.swapaxes(0, 1).reshape(B, S, H * d)


def rglru_mixer(xb, gb, conv_w, conv_b, w_a, b_a, w_x, b_x, lam):
    B, S, W = xb.shape
    f32 = jnp.float32
    xc = causal_dwconv(xb, conv_w) + conv_b
    xg = xc.reshape(B, S, C_GROUPS, C_GROUP_DIM)
    r = jax.nn.sigmoid(jnp.einsum('bsgi,gij->bsgj', xg, w_a) + b_a).reshape(B, S, W)
    i = jax.nn.sigmoid(jnp.einsum('bsgi,gij->bsgj', xg, w_x) + b_x).reshape(B, S, W)
    log_a = -LRU_C * r.astype(f32) * jax.nn.softplus(-lam.astype(f32))
    a = jnp.exp(log_a)
    u = jnp.sqrt(-jnp.expm1(2.0 * log_a)) * (i * xc).astype(f32)

    def combine(lhs, rhs):
        a1, u1 = lhs
        a2, u2 = rhs
        return a1 * a2, a2 * u1 + u2

    _, hs = lax.associative_scan(combine, (a, u), axis=1)
    return hs.astype(xb.dtype) * jax.nn.gelu(gb)


def chunk_gated_delta_rule(q, k, v, g, beta):
    B, S, H, dk = q.shape
    dv = v.shape[-1]
    C = DELTA_CHUNK
    N = S // C

    def chunks(t):
        return jnp.moveaxis(t.reshape(B, N, C, H, *t.shape[3:]), 3, 1)

    q, k, v, g, beta = (chunks(t) for t in (q, k, v, g, beta))
    g = jnp.cumsum(g, axis=-1)
    causal = jnp.tril(jnp.ones((C, C), dtype=bool))
    strict = jnp.tril(jnp.ones((C, C), dtype=bool), -1)
    decay = jnp.exp(jnp.where(causal, g[..., :, None] - g[..., None, :], -jnp.inf))
    kb = k * beta[..., None]
    low = jnp.where(strict, jnp.einsum('bhncd,bhnmd->bhncm', kb, k) * decay, 0.0)
    rhs = jnp.concatenate([v * beta[..., None], kb * jnp.exp(g)[..., None]], axis=-1)
    sol = lax.linalg.triangular_solve(low, rhs, left_side=True, lower=True, unit_diagonal=True)
    u, w = sol[..., :dv], sol[..., dv:]
    qk = jnp.einsum('bhncd,bhnmd->bhncm', q, k) * decay
    q_dec = q * jnp.exp(g)[..., None]
    g_last = g[..., -1]
    k_dec = k * jnp.exp(g_last[..., None] - g)[..., None]

    def step(state, inp):
        qd, kd, ui, wi, qki, gl = inp
        v_new = ui - jnp.einsum('bhcd,bhde->bhce', wi, state)
        o = jnp.einsum('bhcd,bhde->bhce', qd, state) + jnp.einsum('bhcm,bhme->bhce', qki, v_new)
        state = state * jnp.exp(gl)[..., None, None] + jnp.einsum('bhcd,bhce->bhde', kd, v_new)
        return state, o

    xs = tuple(jnp.moveaxis(t, 2, 0) for t in (q_dec, k_dec, u, w, qk, g_last))
    state0 = jnp.zeros((B, H, dk, dv), jnp.float32)
    _, o = lax.scan(step, state0, xs)
    return jnp.transpose(o, (1, 0, 3, 2, 4)).reshape(B, S, H, dv)


def gated_deltanet_mixer(q, k, v, z, beta_logit, a_logit, conv_w, A_log, dt_bias, norm_g):
    B, S, _ = q.shape
    f32 = jnp.float32
    qkv = jax.nn.silu(causal_dwconv(jnp.concatenate([q, k, v], axis=-1), conv_w))
    q, k, v = jnp.split(qkv.astype(f32), 3, axis=-1)
    q = l2norm(q.reshape(B, S, D_HEADS, D_HEAD_DIM)) * D_HEAD_DIM ** -0.5
    k = l2norm(k.reshape(B, S, D_HEADS, D_HEAD_DIM))
    v = v.reshape(B, S, D_HEADS, D_HEAD_DIM)
    beta = jax.nn.sigmoid(beta_logit.astype(f32))
    g = -jnp.exp(A_log.astype(f32)) * jax.nn.softplus(a_logit.astype(f32) + dt_bias.astype(f32))
    o = chunk_gated_delta_rule(q, k, v, g, beta)
    o = rmsnorm(o, norm_g) * jax.nn.silu(z.astype(f32).reshape(B, S, D_HEADS, D_HEAD_DIM))
    return o.reshape(B, S, -1).astype(z.dtype)


def swiglu(x, w1, w3, w2):
    return (jax.nn.silu(x @ w1) * (x @ w3)) @ w2


def moe_swiglu(x, w_router, w1, w3, w2):
    logits = jnp.einsum('bsd,de->bse', x, w_router).astype(jnp.float32)
    top_v, top_i = lax.top_k(logits, TOP_K)
    top_p = jax.nn.softmax(top_v, axis=-1)
    gates = jnp.sum(jax.nn.one_hot(top_i, N_EXPERTS, dtype=jnp.float32) * top_p[..., None], axis=-2)
    out = jnp.zeros_like(x)
    for e in range(N_EXPERTS):
        out = out + gates[..., e:e + 1].astype(x.dtype) * swiglu(x, w1[e], w3[e], w2[e])
    return out


def setup_inputs(seed: int = 0) -> dict:
    key = jax.random.key(seed)
    ks = iter(jax.random.split(key, 48))
    f32 = jnp.float32
    L = DEPTH

    def nrm(shape, scale):
        return jax.random.normal(next(ks), shape, f32) * scale

    def gain(shape):
        return 1.0 + 0.02 * jax.random.normal(next(ks), shape, f32)

    x = nrm((BATCH, SEQ, D_MODEL), 1.0)
    norm_mix = gain((L, D_MODEL))
    norm_ffn = gain((L, D_MODEL))
    norm_final = gain((D_MODEL,))
    w_in = nrm((L, D_MODEL, IN_COLS), D_MODEL ** -0.5)
    lam_q1 = nrm((L, A_QK_DIM), 0.1)
    lam_k1 = nrm((L, A_QK_DIM), 0.1)
    lam_q2 = nrm((L, A_QK_DIM), 0.1)
    lam_k2 = nrm((L, A_QK_DIM), 0.1)
    a_subln = gain((L, A_V_DIM))
    c_conv_w = nrm((L, C_CONV, C_WIDTH), C_CONV ** -0.5)
    c_conv_b = nrm((L, C_WIDTH), 0.01)
    c_w_a = nrm((L, C_GROUPS, C_GROUP_DIM, C_GROUP_DIM), C_GROUP_DIM ** -0.5)
    c_b_a = nrm((L, C_GROUPS, C_GROUP_DIM), 0.01)
    c_w_x = nrm((L, C_GROUPS, C_GROUP_DIM, C_GROUP_DIM), C_GROUP_DIM ** -0.5)
    c_b_x = nrm((L, C_GROUPS, C_GROUP_DIM), 0.01)
    a0 = jax.random.uniform(next(ks), (L, C_WIDTH), f32, 0.9, 0.999) ** (1.0 / LRU_C)
    c_lambda = jnp.log(a0) - jnp.log1p(-a0)
    d_conv_w = nrm((L, D_CONV, 3 * MIX_W), D_CONV ** -0.5)
    d_A_log = jnp.log(jax.random.uniform(next(ks), (L, D_HEADS), f32, 1.0, 16.0))
    dt = jnp.exp(jax.random.uniform(next(ks), (L, D_HEADS), f32, math.log(1e-3), math.log(1e-1)))
    d_dt_bias = dt + jnp.log(-jnp.expm1(-dt))
    d_norm = gain((L, D_HEAD_DIM))
    w_gate = nrm((L, D_MODEL, N_BRANCH, D_MODEL), D_MODEL ** -0.5)
    b_gate = nrm((L, N_BRANCH, D_MODEL), 0.01)
    w_branch = nrm((L, N_BRANCH, MIX_W, D_MODEL), MIX_W ** -0.5)
    w_out = nrm((L, D_MODEL, D_MODEL), D_MODEL ** -0.5)
    ffn_w1 = nrm((N_DENSE, D_MODEL, FFN_DENSE), D_MODEL ** -0.5)
    ffn_w3 = nrm((N_DENSE, D_MODEL, FFN_DENSE), D_MODEL ** -0.5)
    ffn_w2 = nrm((N_DENSE, FFN_DENSE, D_MODEL), FFN_DENSE ** -0.5)
    router = nrm((N_MOE, D_MODEL, N_EXPERTS), D_MODEL ** -0.5)
    moe_w1 = nrm((N_MOE, N_EXPERTS, D_MODEL, FFN_EXPERT), D_MODEL ** -0.5)
    moe_w3 = nrm((N_MOE, N_EXPERTS, D_MODEL, FFN_EXPERT), D_MODEL ** -0.5)
    moe_w2 = nrm((N_MOE, N_EXPERTS, FFN_EXPERT, D_MODEL), FFN_EXPERT ** -0.5)
    return {"x": x, "norm_mix": norm_mix, "norm_ffn": norm_ffn, "norm_final": norm_final,
            "w_in": w_in, "lam_q1": lam_q1, "lam_k1": lam_k1, "lam_q2": lam_q2, "lam_k2": lam_k2,
            "a_subln": a_subln, "c_conv_w": c_conv_w, "c_conv_b": c_conv_b, "c_w_a": c_w_a,
            "c_b_a": c_b_a, "c_w_x": c_w_x, "c_b_x": c_b_x, "c_lambda": c_lambda,
            "d_conv_w": d_conv_w, "d_A_log": d_A_log, "d_dt_bias": d_dt_bias, "d_norm": d_norm,
            "w_gate": w_gate, "b_gate": b_gate, "w_branch": w_branch, "w_out": w_out,
            "ffn_w1": ffn_w1, "ffn_w3": ffn_w3, "ffn_w2": ffn_w2, "router": router,
            "moe_w1": moe_w1, "moe_w3": moe_w3, "moe_w2": moe_w2}


def reference(x, norm_mix, norm_ffn, norm_final, w_in, lam_q1, lam_k1, lam_q2, lam_k2, a_subln,
              c_conv_w, c_conv_b, c_w_a, c_b_a, c_w_x, c_b_x, c_lambda, d_conv_w, d_A_log,
              d_dt_bias, d_norm, w_gate, b_gate, w_branch, w_out, ffn_w1, ffn_w3, ffn_w2, router,
              moe_w1, moe_w3, moe_w2):
    B, S, _ = x.shape
    h = x
    for l in range(DEPTH):
        xn = rmsnorm(h, norm_mix[l])
        (aq, ak, av, bq, bk, bv, cx, cg, dq, dk, dv, dz, dbeta, da) = split_in_proj(xn @ w_in[l])
        o_a = diff_attention(aq.reshape(B, S, A_HEADS, 2, A_QK_DIM), ak.reshape(B, S, A_HEADS, 2, A_QK_DIM),
                             av.reshape(B, S, A_HEADS, A_V_DIM), lam_q1[l], lam_k1[l], lam_q2[l],
                             lam_k2[l], a_subln[l], l)
        o_b = moba_attention(bq.reshape(B, S, B_HEADS, B_HEAD_DIM), bk.reshape(B, S, B_HEADS, B_HEAD_DIM),
                             bv.reshape(B, S, B_HEADS, B_HEAD_DIM))
        o_c = rglru_mixer(cx, cg, c_conv_w[l], c_conv_b[l], c_w_a[l], c_b_a[l], c_w_x[l], c_b_x[l],
                          c_lambda[l])
        o_d = gated_deltanet_mixer(dq, dk, dv, dz, dbeta, da, d_conv_w[l], d_A_log[l], d_dt_bias[l],
                                   d_norm[l])
        branches = jnp.stack([o_a, o_b, o_c, o_d], axis=2)
        y_br = jnp.einsum('bsnw,nwd->bsnd', branches, w_branch[l])
        gates = jax.nn.sigmoid(jnp.einsum('bsd,dne->bsne', xn, w_gate[l]) + b_gate[l])
        h = h + jnp.sum(gates * y_br, axis=2) @ w_out[l]
        xn = rmsnorm(h, norm_ffn[l])
        if l % 2 == 0:
            h = h + swiglu(xn, ffn_w1[l // 2], ffn_w3[l // 2], ffn_w2[l // 2])
        else:
            h = h + moe_swiglu(xn, router[l // 2], moe_w1[l // 2], moe_w3[l // 2], moe_w2[l // 2])
    return rmsnorm(h, norm_final)
```

```python
import functools
import math

import jax
import jax.numpy as jnp
from jax import lax
from jax.experimental import pallas as pl
from jax.experimental.pallas import tpu as pltpu

F32 = jnp.float32
BF16 = jnp.bfloat16
HIGHEST = lax.Precision.HIGHEST

EPS = 1e-6
MIX_W = 512
HEADS = 4
HEAD_DIM = MIX_W // HEADS
A_QK_DIM = HEAD_DIM // 2
MOBA_BLOCK = 256
MOBA_TOPK = 3
C_GROUPS = 8
CONV_K = 4
LRU_C = 8.0
N_EXPERTS = 8
LANES = 128
CONV_HALO = 8
VMEM_LIMIT = 56 * 1024 * 1024


def _cparams(*sem):
    return pltpu.CompilerParams(dimension_semantics=sem, vmem_limit_bytes=VMEM_LIMIT)


def _dot(a, b, precision=None):
    return jnp.dot(a, b, preferred_element_type=F32, precision=precision)


def _dot_nt(a, b, precision=None):
    return lax.dot_general(a, b, (((1,), (1,)), ((), ())), preferred_element_type=F32,
                           precision=precision)


def _dot_tn(a, b, precision=None):
    return lax.dot_general(a, b, (((0,), (0,)), ((), ())), preferred_element_type=F32,
                           precision=precision)


def _silu(x):
    return x * jax.nn.sigmoid(x)


def _softplus(x):
    return jnp.maximum(x, 0.0) + jnp.log1p(jnp.exp(-jnp.abs(x)))


def _norm_proj_kernel(x_ref, g_ref, w_ref, b_ref, o_ref, *, act, precise):
    x = x_ref[...]
    xn = x * lax.rsqrt(jnp.mean(x * x, axis=-1, keepdims=True) + EPS) * g_ref[...]
    if precise:
        y = _dot(xn, w_ref[...], HIGHEST)
    else:
        y = _dot(xn.astype(BF16), w_ref[...])
    y = y + b_ref[...]
    if act == "sigmoid":
        y = jax.nn.sigmoid(y)
    o_ref[...] = y.astype(o_ref.dtype)


def _norm_proj(x, g, w, b, *, out_dtype, tm, tn, act=None, precise=False):
    t, d = x.shape
    n = w.shape[1]
    return pl.pallas_call(
        functools.partial(_norm_proj_kernel, act=act, precise=precise),
        grid=(n // tn, t // tm),
        in_specs=[pl.BlockSpec((tm, d), lambda j, i: (i, 0)),
                  pl.BlockSpec((1, d), lambda j, i: (0, 0)),
                  pl.BlockSpec((d, tn), lambda j, i: (0, j)),
                  pl.BlockSpec((1, tn), lambda j, i: (0, j))],
        out_specs=pl.BlockSpec((tm, tn), lambda j, i: (i, j)),
        out_shape=jax.ShapeDtypeStruct((t, n), out_dtype),
        compiler_params=_cparams("parallel", "parallel"),
        name="norm_proj",
    )(x, g, w, b)


def _softmax_step(s, v, m, l, acc):
    m_new = jnp.maximum(m, jnp.max(s, axis=-1, keepdims=True))
    alpha = jnp.exp(m - m_new)
    p = jnp.exp(s - m_new)
    l = alpha * l + jnp.sum(p, axis=-1, keepdims=True)
    acc = alpha * acc + _dot(p.astype(BF16), v)
    return m_new, l, acc


def _diff_attn_kernel(q_ref, k_ref, v_ref, lam_ref, g_ref, o_ref, *, tq, lam_init):
    i = pl.program_id(2)
    q = q_ref[0]
    lane = lax.broadcasted_iota(jnp.int32, q.shape, 1)
    scale = A_QK_DIM ** -0.5
    zero = jnp.zeros_like(q)
    q0 = jnp.where(lane < A_QK_DIM, q, zero) * scale
    q1 = jnp.where(lane >= A_QK_DIM, q, zero) * scale

    row = lax.broadcasted_iota(jnp.int32, (tq, tq), 0)
    col = lax.broadcasted_iota(jnp.int32, (tq, tq), 1)
    causal = col <= row

    def tile(j, masked, carry):
        start = pl.multiple_of(j * tq, tq)
        kt = k_ref[0, pl.ds(start, tq), :]
        vt = v_ref[0, pl.ds(start, tq), :]
        s0 = _dot_nt(q0, kt)
        s1 = _dot_nt(q1, kt)
        if masked:
            s0 = jnp.where(causal, s0, -jnp.inf)
            s1 = jnp.where(causal, s1, -jnp.inf)
        m0, l0, a0, m1, l1, a1 = carry
        m0, l0, a0 = _softmax_step(s0, vt, m0, l0, a0)
        m1, l1, a1 = _softmax_step(s1, vt, m1, l1, a1)
        return m0, l0, a0, m1, l1, a1

    neg = jnp.full((tq, 1), -jnp.inf, F32)
    zl = jnp.zeros((tq, 1), F32)
    za = jnp.zeros((tq, HEAD_DIM), F32)
    carry = tile(i, True, (neg, zl, za, neg, zl, za))
    carry = lax.fori_loop(0, i, lambda j, c: tile(j, False, c), carry)
    m0, l0, a0, m1, l1, a1 = carry

    lv = lam_ref[...]
    lam = (jnp.exp(jnp.sum(lv[0:1] * lv[1:2], axis=-1, keepdims=True))
           - jnp.exp(jnp.sum(lv[2:3] * lv[3:4], axis=-1, keepdims=True)) + lam_init)
    o = a0 / l0 - lam * (a1 / l1)
    o = o * lax.rsqrt(jnp.mean(o * o, axis=-1, keepdims=True) + 1e-5) * g_ref[...]
    o_ref[0] = (o * (1.0 - lam_init)).astype(o_ref.dtype)


def _diff_attention(qkv, col0, lam_vecs, sub_g, layer_idx, *, tq=256):
    bsz, s, _ = qkv.shape
    lam_init = 0.8 - 0.6 * math.exp(-0.3 * layer_idx)
    return pl.pallas_call(
        functools.partial(_diff_attn_kernel, tq=tq, lam_init=lam_init),
        grid=(bsz, HEADS, s // tq),
        in_specs=[pl.BlockSpec((1, tq, HEAD_DIM), lambda b, h, i: (b, i, col0 + h)),
                  pl.BlockSpec((1, s, HEAD_DIM), lambda b, h, i: (b, 0, col0 + HEADS + h)),
                  pl.BlockSpec((1, s, HEAD_DIM), lambda b, h, i: (b, 0, col0 + 2 * HEADS + h)),
                  pl.BlockSpec((4, A_QK_DIM), lambda b, h, i: (0, 0)),
                  pl.BlockSpec((1, HEAD_DIM), lambda b, h, i: (0, 0))],
        out_specs=pl.BlockSpec((1, tq, HEAD_DIM), lambda b, h, i: (b, i, h)),
        out_shape=jax.ShapeDtypeStruct((bsz, s, MIX_W), BF16),
        compiler_params=_cparams("parallel", "parallel", "arbitrary"),
        name="diff_attention",
    )(qkv, qkv, qkv, lam_vecs, sub_g)


def _moba_kernel(q_ref, k_ref, v_ref, o_ref, *, n_blocks):
    own = pl.program_id(2)
    blk = MOBA_BLOCK
    q = q_ref[0]
    scale = HEAD_DIM ** -0.5

    kmean = jnp.mean(k_ref[0].astype(F32).reshape(n_blocks, blk, HEAD_DIM), axis=1)
    kmean = jnp.concatenate([kmean, jnp.zeros((LANES - n_blocks, HEAD_DIM), F32)], axis=0)
    gate = _dot_nt(q.astype(F32), kmean, HIGHEST)

    lane = lax.broadcasted_iota(jnp.int32, (blk, LANES), 1)
    valid = lane < own
    gv = jnp.where(valid, gate, -jnp.inf)
    rank = jnp.zeros((blk, LANES), F32)
    for jp in range(n_blocks - 1):
        gj = gv[:, jp:jp + 1]
        ahead = (gj > gv) | ((gj == gv) & (jp < lane))
        rank = rank + jnp.where(ahead, 1.0, 0.0)
    topk = min(MOBA_TOPK, n_blocks - 1)
    sel = jnp.where(valid & (rank < float(topk)), 1.0, 0.0)

    row = lax.broadcasted_iota(jnp.int32, (blk, blk), 0)
    col = lax.broadcasted_iota(jnp.int32, (blk, blk), 1)

    def tile(j, keep, carry):
        start = pl.multiple_of(j * blk, blk)
        kt = k_ref[0, pl.ds(start, blk), :]
        vt = v_ref[0, pl.ds(start, blk), :]
        s = _dot_nt(q, kt) * scale
        s = jnp.where(keep, s, -jnp.inf)
        return _softmax_step(s, vt, *carry)

    neg = jnp.full((blk, 1), -jnp.inf, F32)
    carry = tile(own, col <= row, (neg, jnp.zeros((blk, 1), F32), jnp.zeros((blk, HEAD_DIM), F32)))

    def past(j, carry):
        sel_j = jnp.max(jnp.where(lane == j, sel, 0.0), axis=-1, keepdims=True)
        return tile(j, sel_j > 0.0, carry)

    m, l, acc = lax.fori_loop(0, own, past, carry)
    o_ref[0] = (acc / l).astype(o_ref.dtype)


def _moba_attention(qkv, col0):
    bsz, s, _ = qkv.shape
    n_blocks = s // MOBA_BLOCK
    return pl.pallas_call(
        functools.partial(_moba_kernel, n_blocks=n_blocks),
        grid=(bsz, HEADS, n_blocks),
        in_specs=[pl.BlockSpec((1, MOBA_BLOCK, HEAD_DIM), lambda b, h, i: (b, i, col0 + h)),
                  pl.BlockSpec((1, s, HEAD_DIM), lambda b, h, i: (b, 0, col0 + HEADS + h)),
                  pl.BlockSpec((1, s, HEAD_DIM), lambda b, h, i: (b, 0, col0 + 2 * HEADS + h))],
        out_specs=pl.BlockSpec((1, MOBA_BLOCK, HEAD_DIM), lambda b, h, i: (b, i, h)),
        out_shape=jax.ShapeDtypeStruct((bsz, s, MIX_W), BF16),
        compiler_params=_cparams("parallel", "parallel", "arbitrary"),
        name="moba_attention",
    )(qkv, qkv, qkv)


def _causal_conv(x, halo, first, w, xe_ref):
    ts = x.shape[0]
    xe_ref[0:CONV_HALO, :] = jnp.where(first, jnp.zeros_like(halo), halo)
    xe_ref[CONV_HALO:CONV_HALO + ts, :] = x
    out = w[CONV_K - 1:CONV_K] * x
    for k in range(CONV_K - 1):
        off = CONV_HALO - (CONV_K - 1) + k
        out = out + w[k:k + 1] * xe_ref[off:off + ts, :]
    return out


def _rglru_kernel(x_ref, xh_ref, gb_ref, cw_ref, cb_ref, wbd_ref, bbd_ref, lam_ref, o_ref,
                  xe_ref, h_ref, *, ts):
    i = pl.program_id(1)
    width = x_ref.shape[-1]
    xc = _causal_conv(x_ref[0], xh_ref[0], i == 0, cw_ref[...], xe_ref) + cb_ref[...]
    ri = jax.nn.sigmoid(_dot(xc.astype(BF16), wbd_ref[...]) + bbd_ref[...])
    r = ri[:, :width]
    ig = ri[:, width:]
    log_a = -LRU_C * r * _softplus(-lam_ref[...])
    a = jnp.exp(log_a)
    th = jnp.tanh(log_a)
    u = jnp.sqrt(-2.0 * th / (1.0 - th)) * (ig * xc)

    rows = lax.broadcasted_iota(jnp.int32, (ts, width), 0)
    d = 1
    while d < ts:
        keep = rows >= d
        a_sh = jnp.where(keep, pltpu.roll(a, d, axis=0), 1.0)
        u_sh = jnp.where(keep, pltpu.roll(u, d, axis=0), 0.0)
        u = a * u_sh + u
        a = a * a_sh
        d *= 2

    @pl.when(i == 0)
    def _():
        h_ref[...] = jnp.zeros_like(h_ref)

    hs = u + a * h_ref[0:1, :]
    h_ref[0:1, :] = hs[ts - 1:ts, :]
    o_ref[0] = (hs * jax.nn.gelu(gb_ref[0])).astype(o_ref.dtype)


def _rglru(cd, col0, conv_w, conv_b, wbd, bbd, lam, *, ts=256):
    bsz, s, _ = cd.shape
    w = MIX_W
    hb = ts // CONV_HALO
    const = lambda b, i: (0, 0)
    return pl.pallas_call(
        functools.partial(_rglru_kernel, ts=ts),
        grid=(bsz, s // ts),
        in_specs=[pl.BlockSpec((1, ts, w), lambda b, i: (b, i, col0)),
                  pl.BlockSpec((1, CONV_HALO, w), lambda b, i: (b, jnp.maximum(i * hb - 1, 0), col0)),
                  pl.BlockSpec((1, ts, w), lambda b, i: (b, i, col0 + 1)),
                  pl.BlockSpec((CONV_K, w), const),
                  pl.BlockSpec((1, w), const),
                  pl.BlockSpec((w, 2 * w), const),
                  pl.BlockSpec((1, 2 * w), const),
                  pl.BlockSpec((1, w), const)],
        out_specs=pl.BlockSpec((1, ts, w), lambda b, i: (b, i, 0)),
        out_shape=jax.ShapeDtypeStruct((bsz, s, w), BF16),
        scratch_shapes=[pltpu.VMEM((ts + CONV_HALO, w), F32), pltpu.VMEM((8, w), F32)],
        compiler_params=_cparams("parallel", "arbitrary"),
        name="rglru",
    )(cd, cd, cd, conv_w, conv_b, wbd, bbd, lam)


def _unit_lower_inverse(m, ts):
    row = lax.broadcasted_iota(jnp.int32, (ts, ts), 0)
    col = lax.broadcasted_iota(jnp.int32, (ts, ts), 1)
    x = jnp.where(row == col, 1.0, 0.0).astype(F32)
    sh = 0
    while (1 << sh) < ts:
        same_big = (row >> (sh + 1)) == (col >> (sh + 1))
        c = jnp.where(same_big & ((row >> sh) > (col >> sh)), m, 0.0)
        cx = _dot(c.astype(BF16), x.astype(BF16))
        x = x - _dot(x.astype(BF16), cx.astype(BF16))
        sh += 1
    return x


def _delta_prep_kernel(q_ref, qh_ref, k_ref, kh_ref, v_ref, vh_ref, sc_ref, sr_ref, cw_ref,
                       hc_ref, hr_ref,
                       qd_ref, kd_ref, u_ref, w_ref, qk_ref, egl_ref, xe_ref, *, ts):
    first = pl.program_id(1) == 0
    cw = cw_ref[...]
    q = _silu(_causal_conv(q_ref[0], qh_ref[0], first, cw[:, 0:MIX_W], xe_ref))
    k = _silu(_causal_conv(k_ref[0], kh_ref[0], first, cw[:, MIX_W:2 * MIX_W], xe_ref))
    v = _silu(_causal_conv(v_ref[0], vh_ref[0], first, cw[:, 2 * MIX_W:3 * MIX_W], xe_ref))

    sc = sc_ref[0]
    hc = hc_ref[...]
    beta_c = jax.nn.sigmoid(sc)
    g_c = -hc[0:1] * _softplus(sc + hc[1:2])
    sr = sr_ref[0]
    hr = hr_ref[...]
    g_r = -hr[:, 0:1] * _softplus(sr + hr[:, 1:2])

    row = lax.broadcasted_iota(jnp.int32, (ts, ts), 0)
    col = lax.broadcasted_iota(jnp.int32, (ts, ts), 1)
    lower_incl = jnp.where(col <= row, 1.0, 0.0).astype(F32)
    gc_c = _dot(lower_incl, g_c, HIGHEST)
    gc_r = _dot_nt(g_r, lower_incl, HIGHEST)

    egl_ref[...] = jnp.zeros_like(egl_ref)
    for h in range(HEADS):
        sl = slice(h * HEAD_DIM, (h + 1) * HEAD_DIM)
        qh = q[:, sl]
        kh = k[:, sl]
        vh = v[:, sl]
        qh = qh * lax.rsqrt(jnp.sum(qh * qh, axis=-1, keepdims=True) + 1e-6) * HEAD_DIM ** -0.5
        kh = kh * lax.rsqrt(jnp.sum(kh * kh, axis=-1, keepdims=True) + 1e-6)
        beta = beta_c[:, h:h + 1]
        gcol = gc_c[:, HEADS + h:HEADS + h + 1]
        grow = gc_r[HEADS + h:HEADS + h + 1, :]
        glast = gcol[ts - 1:ts, :]
        decay = jnp.exp(jnp.where(col <= row, gcol - grow, -jnp.inf))
        kb = kh * beta
        kk = _dot_nt(kb.astype(BF16), kh.astype(BF16))
        m = jnp.where(row == col, 1.0, jnp.where(col < row, kk * decay, 0.0))
        x = _unit_lower_inverse(m, ts).astype(BF16)
        eg = jnp.exp(gcol)
        u = _dot(x, (vh * beta).astype(BF16))
        w = _dot(x, (kb * eg).astype(BF16))
        qk = _dot_nt(qh.astype(BF16), kh.astype(BF16)) * decay
        qd_ref[0, :, sl] = (qh * eg).astype(qd_ref.dtype)
        kd_ref[0, :, sl] = (kh * jnp.exp(glast - gcol)).astype(kd_ref.dtype)
        u_ref[0, :, sl] = u
        w_ref[0, :, sl] = w.astype(w_ref.dtype)
        qk_ref[0, h] = qk.astype(qk_ref.dtype)
        egl_ref[0, 0, h:h + 1, :] = jnp.broadcast_to(jnp.exp(glast), (1, LANES))


def _delta_scan_kernel(qd_ref, kd_ref, u_ref, w_ref, qk_ref, egl_ref, z_ref, g_ref, o_ref, st_ref):
    @pl.when(pl.program_id(1) == 0)
    def _():
        st_ref[...] = jnp.zeros_like(st_ref)

    for h in range(HEADS):
        sl = slice(h * HEAD_DIM, (h + 1) * HEAD_DIM)
        state = st_ref[h]
        sb = state.astype(BF16)
        v_new = u_ref[0, :, sl] - _dot(w_ref[0, :, sl], sb)
        vb = v_new.astype(BF16)
        o = _dot(qd_ref[0, :, sl], sb) + _dot(qk_ref[0, h], vb)
        st_ref[h] = state * egl_ref[0, 0, h:h + 1, 0:1] + _dot_tn(kd_ref[0, :, sl], vb)
        o = o * lax.rsqrt(jnp.mean(o * o, axis=-1, keepdims=True) + EPS) * g_ref[...]
        o_ref[0, :, sl] = (o * _silu(z_ref[0, :, sl])).astype(o_ref.dtype)


def _gated_deltanet(cd, col0, small, conv_w, a_log, dt_bias, norm_g, *, ts=256):
    bsz, s, _ = cd.shape
    w = MIX_W
    nc = s // ts
    hb = ts // CONV_HALO
    small_rows = jnp.swapaxes(small[:, :, :2 * HEADS], 1, 2)
    ea = jnp.exp(a_log.astype(F32))
    head_cols = jnp.zeros((2, LANES), F32)
    head_cols = head_cols.at[0, HEADS:2 * HEADS].set(ea).at[1, HEADS:2 * HEADS].set(dt_bias)
    head_rows = jnp.zeros((2 * HEADS, 2), F32)
    head_rows = head_rows.at[HEADS:, 0].set(ea).at[HEADS:, 1].set(dt_bias)

    def tile_spec(c):
        return pl.BlockSpec((1, ts, w), lambda b, i: (b, i, col0 + c))

    def halo_spec(c):
        return pl.BlockSpec((1, CONV_HALO, w), lambda b, i: (b, jnp.maximum(i * hb - 1, 0), col0 + c))

    const = lambda b, i: (0, 0)
    row_out = pl.BlockSpec((1, ts, w), lambda b, i: (b, i, 0))
    qd, kd, u, wm, qk, egl = pl.pallas_call(
        functools.partial(_delta_prep_kernel, ts=ts),
        grid=(bsz, nc),
        in_specs=[tile_spec(0), halo_spec(0), tile_spec(1), halo_spec(1), tile_spec(2), halo_spec(2),
                  pl.BlockSpec((1, ts, LANES), lambda b, i: (b, i, 0)),
                  pl.BlockSpec((1, 2 * HEADS, ts), lambda b, i: (b, 0, i)),
                  pl.BlockSpec((CONV_K, 3 * w), const),
                  pl.BlockSpec((2, LANES), const),
                  pl.BlockSpec((2 * HEADS, 2), const)],
        out_specs=[row_out, row_out, row_out, row_out,
                   pl.BlockSpec((1, HEADS, ts, ts), lambda b, i: (b, 0, i, 0)),
                   pl.BlockSpec((1, 1, 8, LANES), lambda b, i: (b, i, 0, 0))],
        out_shape=[jax.ShapeDtypeStruct((bsz, s, w), BF16),
                   jax.ShapeDtypeStruct((bsz, s, w), BF16),
                   jax.ShapeDtypeStruct((bsz, s, w), F32),
                   jax.ShapeDtypeStruct((bsz, s, w), BF16),
                   jax.ShapeDtypeStruct((bsz, HEADS, s, ts), BF16),
                   jax.ShapeDtypeStruct((bsz, nc, 8, LANES), F32)],
        scratch_shapes=[pltpu.VMEM((ts + CONV_HALO, w), F32)],
        compiler_params=_cparams("parallel", "parallel"),
        name="delta_prep",
    )(cd, cd, cd, cd, cd, cd, small, small_rows, conv_w, head_cols, head_rows)

    return pl.pallas_call(
        _delta_scan_kernel,
        grid=(bsz, nc),
        in_specs=[row_out, row_out, row_out, row_out,
                  pl.BlockSpec((1, HEADS, ts, ts), lambda b, i: (b, 0, i, 0)),
                  pl.BlockSpec((1, 1, 8, LANES), lambda b, i: (b, i, 0, 0)),
                  tile_spec(3),
                  pl.BlockSpec((1, HEAD_DIM), const)],
        out_specs=row_out,
        out_shape=jax.ShapeDtypeStruct((bsz, s, w), BF16),
        scratch_shapes=[pltpu.VMEM((HEADS, HEAD_DIM, HEAD_DIM), F32)],
        compiler_params=_cparams("parallel", "arbitrary"),
        name="delta_scan",
    )(qd, kd, u, wm, qk, egl, cd, norm_g)


def _top2_gates(logits):
    lane = lax.broadcasted_iota(jnp.int32, logits.shape, 1)
    big = jnp.int32(LANES)
    v1 = jnp.max(logits, axis=-1, keepdims=True)
    i1 = jnp.min(jnp.where(logits == v1, lane, big), axis=-1, keepdims=True)
    rest = jnp.where(lane == i1, -jnp.inf, logits)
    v2 = jnp.max(rest, axis=-1, keepdims=True)
    i2 = jnp.min(jnp.where(rest == v2, lane, big), axis=-1, keepdims=True)
    e2 = jnp.exp(v2 - v1)
    p1 = 1.0 / (1.0 + e2)
    p2 = e2 / (1.0 + e2)
    return jnp.where(lane == i1, p1, jnp.where(lane == i2, p2, 0.0))


def _merge_kernel(oa_ref, ob_ref, oc_ref, od_ref, gate_ref, h_ref, wb_ref, wo_ref, g_ref, wr_ref,
                  h_out_ref, xn_ref, route_ref, *, route):
    d = h_ref.shape[-1]
    y = None
    for n, o_ref in enumerate((oa_ref, ob_ref, oc_ref, od_ref)):
        yb = _dot(o_ref[...], wb_ref[n]) * gate_ref[:, n * d:(n + 1) * d].astype(F32)
        y = yb if y is None else y + yb
    h = h_ref[...] + _dot(y.astype(BF16), wo_ref[...])
    h_out_ref[...] = h
    xn = h * lax.rsqrt(jnp.mean(h * h, axis=-1, keepdims=True) + EPS) * g_ref[...]
    xn_ref[...] = xn.astype(xn_ref.dtype)
    if route:
        lane = lax.broadcasted_iota(jnp.int32, route_ref.shape, 1)
        logits = jnp.where(lane < N_EXPERTS, _dot(xn, wr_ref[...], HIGHEST), -jnp.inf)
        route_ref[...] = _top2_gates(logits)
    else:
        route_ref[...] = jnp.zeros_like(route_ref)


def _merge(branches, gates, h, wb, wo, g, wr, *, route, tm=512):
    t, d = h.shape
    w = MIX_W
    row = lambda i: (i, 0)
    const2 = lambda i: (0, 0)
    return pl.pallas_call(
        functools.partial(_merge_kernel, route=route),
        grid=(t // tm,),
        in_specs=[pl.BlockSpec((tm, w), row)] * 4 + [
            pl.BlockSpec((tm, 4 * d), row),
            pl.BlockSpec((tm, d), row),
            pl.BlockSpec((4, w, d), lambda i: (0, 0, 0)),
            pl.BlockSpec((d, d), const2),
            pl.BlockSpec((1, d), const2),
            pl.BlockSpec((d, LANES), const2)],
        out_specs=[pl.BlockSpec((tm, d), row), pl.BlockSpec((tm, d), row), pl.BlockSpec((tm, LANES), row)],
        out_shape=[jax.ShapeDtypeStruct((t, d), F32), jax.ShapeDtypeStruct((t, d), BF16),
                   jax.ShapeDtypeStruct((t, LANES), F32)],
        compiler_params=_cparams("parallel"),
        name="merge",
    )(*branches, gates, h, wb, wo, g, wr)


def _ffn_kernel(x_ref, h_ref, gate_ref, w1_ref, w3_ref, w2_ref, gf_ref, o_ref, acc_ref, *,
                gated, final_norm):
    e = pl.program_id(1)
    f = pl.program_id(2)

    @pl.when((e == 0) & (f == 0))
    def _():
        acc_ref[...] = h_ref[...]

    x = x_ref[...]
    a = _silu(_dot(x, w1_ref[0])) * _dot(x, w3_ref[0])
    if gated:
        lane = lax.broadcasted_iota(jnp.int32, gate_ref.shape, 1)
        a = a * jnp.sum(jnp.where(lane == e, gate_ref[...], 0.0), axis=-1, keepdims=True)
    acc_ref[...] += _dot(a.astype(BF16), w2_ref[0])

    @pl.when((e == pl.num_programs(1) - 1) & (f == pl.num_programs(2) - 1))
    def _():
        h = acc_ref[...]
        if final_norm:
            h = h * lax.rsqrt(jnp.mean(h * h, axis=-1, keepdims=True) + EPS) * gf_ref[...]
        o_ref[...] = h


def _ffn(xn, h, gates, w1, w3, w2, g_final, *, gated, final_norm, tm, tf):
    t, d = h.shape
    ne, _, ff = w1.shape
    row = lambda i, e, f: (i, 0)
    return pl.pallas_call(
        functools.partial(_ffn_kernel, gated=gated, final_norm=final_norm),
        grid=(t // tm, ne, ff // tf),
        in_specs=[pl.BlockSpec((tm, d), row),
                  pl.BlockSpec((tm, d), row),
                  pl.BlockSpec((tm, LANES), row),
                  pl.BlockSpec((1, d, tf), lambda i, e, f: (e, 0, f)),
                  pl.BlockSpec((1, d, tf), lambda i, e, f: (e, 0, f)),
                  pl.BlockSpec((1, tf, d), lambda i, e, f: (e, f, 0)),
                  pl.BlockSpec((1, d), lambda i, e, f: (0, 0))],
        out_specs=pl.BlockSpec((tm, d), row),
        out_shape=jax.ShapeDtypeStruct((t, d), F32),
        scratch_shapes=[pltpu.VMEM((tm, d), F32)],
        compiler_params=_cparams("parallel", "arbitrary", "arbitrary"),
        name="ffn",
    )(xn, h, gates, w1, w3, w2, g_final)


def _block_diag(w):
    g, di, dj = w.shape
    eye = jnp.eye(g, dtype=w.dtype)
    return (eye[:, None, :, None] * w[:, :, None, :]).reshape(g * di, g * dj)


def kernel(x, norm_mix, norm_ffn, norm_final, w_in, lam_q1, lam_k1, lam_q2, lam_k2, a_subln, c_conv_w, c_conv_b, c_w_a, c_b_a, c_w_x, c_b_x, c_lambda, d_conv_w, d_A_log, d_dt_bias, d_norm, w_gate, b_gate, w_branch, w_out, ffn_w1, ffn_w3, ffn_w2, router, moe_w1, moe_w3, moe_w2):
    bsz, s, d = x.shape
    t = bsz * s
    depth = w_in.shape[0]
    n_ab = 6 * MIX_W
    n_cd = 6 * MIX_W
    h = x.reshape(t, d)
    out = None
    for l in range(depth):
        g_mix = norm_mix[l].reshape(1, d)
        w_l = w_in[l]
        w_ab = w_l[:, :n_ab].astype(BF16)
        w_cd = w_l[:, n_ab:n_ab + n_cd].astype(BF16)
        w_small = jnp.pad(w_l[:, n_ab + n_cd:], ((0, 0), (0, LANES - 2 * HEADS)))
        ab = _norm_proj(h, g_mix, w_ab, jnp.zeros((1, n_ab), F32), out_dtype=BF16, tm=512, tn=1536)
        cd = _norm_proj(h, g_mix, w_cd, jnp.zeros((1, n_cd), F32), out_dtype=F32, tm=512, tn=1536)
        small = _norm_proj(h, g_mix, w_small, jnp.zeros((1, LANES), F32), out_dtype=F32, tm=512,
                           tn=LANES, precise=True)
        gates = _norm_proj(h, g_mix, w_gate[l].reshape(d, 4 * d).astype(BF16),
                           b_gate[l].reshape(1, 4 * d), out_dtype=BF16, tm=512, tn=1024, act="sigmoid")

        ab3 = ab.reshape(bsz, s, n_ab)
        cd3 = cd.reshape(bsz, s, n_cd)
        lam_vecs = jnp.stack([lam_q1[l], lam_k1[l], lam_q2[l], lam_k2[l]])
        o_a = _diff_attention(ab3, 0, lam_vecs, a_subln[l].reshape(1, HEAD_DIM), l)
        o_b = _moba_attention(ab3, 3 * HEADS)
        wbd = jnp.concatenate([_block_diag(c_w_a[l]), _block_diag(c_w_x[l])], axis=1).astype(BF16)
        bbd = jnp.concatenate([c_b_a[l].reshape(1, MIX_W), c_b_x[l].reshape(1, MIX_W)], axis=1)
        o_c = _rglru(cd3, 0, c_conv_w[l], c_conv_b[l].reshape(1, MIX_W), wbd, bbd,
                     c_lambda[l].reshape(1, MIX_W))
        o_d = _gated_deltanet(cd3, 2, small.reshape(bsz, s, LANES), d_conv_w[l], d_A_log[l],
                              d_dt_bias[l], d_norm[l].reshape(1, HEAD_DIM))

        moe_layer = l % 2 == 1
        wr = (jnp.pad(router[l // 2], ((0, 0), (0, LANES - N_EXPERTS))) if moe_layer
              else jnp.zeros((d, LANES), F32))
        branches = [o.reshape(t, MIX_W) for o in (o_a, o_b, o_c, o_d)]
        h, xn, route = _merge(branches, gates, h, w_branch[l].astype(BF16), w_out[l].astype(BF16),
                              norm_ffn[l].reshape(1, d), wr, route=moe_layer)
        last = l == depth - 1
        g_final = norm_final.reshape(1, d)
        if moe_layer:
            h = _ffn(xn, h, route, moe_w1[l // 2].astype(BF16), moe_w3[l // 2].astype(BF16),
                     moe_w2[l // 2].astype(BF16), g_final, gated=True, final_norm=last, tm=1024, tf=512)
        else:
            h = _ffn(xn, h, route, ffn_w1[l // 2][None].astype(BF16), ffn_w3[l // 2][None].astype(BF16),
                     ffn_w2[l // 2][None].astype(BF16), g_final, gated=False, final_norm=last,
                     tm=512, tf=1408)
        out = h
    return out.reshape(bsz, s, d)
```

```python
import functools
import math

import jax
import jax.numpy as jnp
from jax import lax
from jax.experimental import pallas as pl
from jax.experimental.pallas import tpu as pltpu

F32 = jnp.float32
BF16 = jnp.bfloat16

EPS = 1e-6
MIX_W = 512
HEADS = 4
HEAD_DIM = MIX_W // HEADS
A_QK_DIM = HEAD_DIM // 2
MOBA_BLOCK = 256
MOBA_TOPK = 3
CONV_K = 4
LRU_C = 8.0
N_EXPERTS = 8
LANES = 128
CONV_HALO = 8
VMEM_LIMIT = 56 * 1024 * 1024


def _cparams(*sem):
    return pltpu.CompilerParams(dimension_semantics=sem, vmem_limit_bytes=VMEM_LIMIT)


def _dot(a, b):
    return jnp.dot(a, b, preferred_element_type=F32)


def _dot_nt(a, b):
    return lax.dot_general(a, b, (((1,), (1,)), ((), ())), preferred_element_type=F32)


def _dot_tn(a, b):
    return lax.dot_general(a, b, (((0,), (0,)), ((), ())), preferred_element_type=F32)


def _split_bf16(a):
    hi = a.astype(BF16)
    return hi, (a - hi.astype(F32)).astype(BF16)


def _split3_bf16(a):
    parts = []
    for _ in range(3):
        p = a.astype(BF16)
        parts.append(p)
        a = a - p.astype(F32)
    return parts


def _dot_3pass(a, b):
    ah, al = _split_bf16(a)
    bh, bl = _split_bf16(b)
    return _dot(ah, bh) + (_dot(ah, bl) + _dot(al, bh))


def _silu(x):
    return x * jax.nn.sigmoid(x)


def _softplus(x):
    return jnp.maximum(x, 0.0) + jnp.log1p(jnp.exp(-jnp.abs(x)))


def _norm_proj_kernel(x_ref, g_ref, w_ref, b_ref, o_ref, *, act, precise):
    x = x_ref[...]
    xn = x * lax.rsqrt(jnp.mean(x * x, axis=-1, keepdims=True) + EPS) * g_ref[...]
    if precise:
        y = _dot_3pass(xn, w_ref[...])
    else:
        y = _dot(xn.astype(BF16), w_ref[...])
    y = y + b_ref[...]
    if act == "sigmoid":
        y = jax.nn.sigmoid(y)
    o_ref[...] = y.astype(o_ref.dtype)


def _norm_proj(x, g, w, b, *, out_dtype, tm, tn, act=None, precise=False):
    t, d = x.shape
    n = w.shape[1]
    return pl.pallas_call(
        functools.partial(_norm_proj_kernel, act=act, precise=precise),
        grid=(n // tn, t // tm),
        in_specs=[pl.BlockSpec((tm, d), lambda j, i: (i, 0)),
                  pl.BlockSpec((1, d), lambda j, i: (0, 0)),
                  pl.BlockSpec((d, tn), lambda j, i: (0, j)),
                  pl.BlockSpec((1, tn), lambda j, i: (0, j))],
        out_specs=pl.BlockSpec((tm, tn), lambda j, i: (i, j)),
        out_shape=jax.ShapeDtypeStruct((t, n), out_dtype),
        compiler_params=_cparams("parallel", "parallel"),
        name="norm_proj",
    )(x, g, w, b)


def _two_pass_softmax_pv(n_past, streams, tq, tk):
    chunks = [slice(c * LANES, (c + 1) * LANES) for c in range(tk // LANES)]

    def lane_fold(s, op):
        out = s[:, chunks[0]]
        for c in chunks[1:]:
            out = op(out, s[:, c])
        return out

    s_diag = [st[0]() for st in streams]
    mruns = tuple(lane_fold(s, jnp.maximum) for s in s_diag)

    def pass1(j, mr):
        return tuple(jnp.maximum(m, lane_fold(st[1](j), jnp.maximum)) for m, st in zip(mr, streams))

    mruns = lax.fori_loop(0, n_past, pass1, mruns)
    mbs = [jnp.broadcast_to(jnp.max(m, axis=-1, keepdims=True), (tq, LANES)) for m in mruns]

    def probs(s, mb):
        ps = [jnp.exp(s[:, c] - mb) for c in chunks]
        lsum = ps[0]
        for p in ps[1:]:
            lsum = lsum + p
        return jnp.concatenate(ps, axis=1).astype(BF16), lsum

    carry = []
    for s, mb, st in zip(s_diag, mbs, streams):
        p, lsum = probs(s, mb)
        carry.append((lsum, _dot(p, st[2](n_past))))

    def pass2(j, carry):
        out = []
        for (lr, ac), mb, st in zip(carry, mbs, streams):
            p, lsum = probs(st[1](j), mb)
            out.append((lr + lsum, ac + _dot(p, st[2](j))))
        return tuple(out)

    carry = lax.fori_loop(0, n_past, pass2, tuple(carry))
    return [(ac, jnp.sum(lr, axis=-1, keepdims=True)) for lr, ac in carry]


def _diff_attn_kernel(q_ref, k_ref, v_ref, lam_ref, g_ref, o_ref, *, tq, lam_init):
    i = pl.program_id(2)
    q = q_ref[0]
    lane = lax.broadcasted_iota(jnp.int32, q.shape, 1)
    scale = A_QK_DIM ** -0.5
    zero = jnp.zeros_like(q)
    row = lax.broadcasted_iota(jnp.int32, (tq, tq), 0)
    col = lax.broadcasted_iota(jnp.int32, (tq, tq), 1)
    causal = col <= row

    def tile(ref, j):
        return ref[0, pl.ds(pl.multiple_of(j * tq, tq), tq), :]

    def stream(c):
        in_half = (lane < A_QK_DIM) if c == 0 else (lane >= A_QK_DIM)
        qc = jnp.where(in_half, q, zero) * scale
        return (lambda: jnp.where(causal, _dot_nt(qc, tile(k_ref, i)), -jnp.inf),
                lambda j: _dot_nt(qc, tile(k_ref, j)),
                lambda j: tile(v_ref, j))

    (a0, l0), (a1, l1) = _two_pass_softmax_pv(i, [stream(0), stream(1)], tq, tq)

    lv = lam_ref[...]
    lam = (jnp.exp(jnp.sum(lv[0:1] * lv[1:2], axis=-1, keepdims=True))
           - jnp.exp(jnp.sum(lv[2:3] * lv[3:4], axis=-1, keepdims=True)) + lam_init)
    o = a0 / l0 - lam * (a1 / l1)
    o = o * lax.rsqrt(jnp.mean(o * o, axis=-1, keepdims=True) + 1e-5) * g_ref[...]
    o_ref[0] = (o * (1.0 - lam_init)).astype(o_ref.dtype)


def _diff_attention(qkv, col0, lam_vecs, sub_g, layer_idx, *, tq=512):
    bsz, s, _ = qkv.shape
    lam_init = 0.8 - 0.6 * math.exp(-0.3 * layer_idx)
    return pl.pallas_call(
        functools.partial(_diff_attn_kernel, tq=tq, lam_init=lam_init),
        grid=(bsz, HEADS, s // tq),
        in_specs=[pl.BlockSpec((1, tq, HEAD_DIM), lambda b, h, i: (b, i, col0 + h)),
                  pl.BlockSpec((1, s, HEAD_DIM), lambda b, h, i: (b, 0, col0 + HEADS + h)),
                  pl.BlockSpec((1, s, HEAD_DIM), lambda b, h, i: (b, 0, col0 + 2 * HEADS + h)),
                  pl.BlockSpec((4, A_QK_DIM), lambda b, h, i: (0, 0)),
                  pl.BlockSpec((1, HEAD_DIM), lambda b, h, i: (0, 0))],
        out_specs=pl.BlockSpec((1, tq, HEAD_DIM), lambda b, h, i: (b, i, h)),
        out_shape=jax.ShapeDtypeStruct((bsz, s, MIX_W), BF16),
        compiler_params=_cparams("parallel", "parallel", "arbitrary"),
        name="diff_attention",
    )(qkv, qkv, qkv, lam_vecs, sub_g)


def _moba_kernel(q_ref, k_ref, v_ref, o_ref, kmean_ref, *, n_blocks):
    own = pl.program_id(1)
    blk = MOBA_BLOCK
    scale = HEAD_DIM ** -0.5

    @pl.when(own == 0)
    def _():
        kmean_ref[...] = jnp.zeros_like(kmean_ref)
        for h in range(HEADS):
            kh = k_ref[0, :, h * HEAD_DIM:(h + 1) * HEAD_DIM].astype(F32)
            kmean_ref[h, 0:n_blocks, :] = jnp.mean(kh.reshape(n_blocks, blk, HEAD_DIM), axis=1)

    lane = lax.broadcasted_iota(jnp.int32, (blk, LANES), 1)
    valid = lane < own
    row = lax.broadcasted_iota(jnp.int32, (blk, blk), 0)
    col = lax.broadcasted_iota(jnp.int32, (blk, blk), 1)
    topk = min(MOBA_TOPK, n_blocks - 1)

    def tile(ref, j, hs):
        return ref[0, pl.ds(pl.multiple_of(j * blk, blk), blk), hs]

    def stream(h):
        hs = slice(h * HEAD_DIM, (h + 1) * HEAD_DIM)
        q = q_ref[0, :, hs]
        km_hi, km_lo = _split_bf16(kmean_ref[h])
        gate = _dot_nt(q, km_hi) + _dot_nt(q, km_lo)
        gv = jnp.where(valid, gate, -jnp.inf)
        rank = jnp.zeros((blk, LANES), F32)
        for jp in range(n_blocks - 1):
            gj = gv[:, jp:jp + 1]
            ahead = (gj > gv) | ((gj == gv) & (jp < lane))
            rank = rank + jnp.where(ahead, 1.0, 0.0)
        sel = jnp.where(valid & (rank < float(topk)), 1.0, 0.0)

        def past_scores(j):
            sel_j = jnp.max(jnp.where(lane == j, sel, 0.0), axis=-1, keepdims=True)
            return jnp.where(sel_j > 0.0, _dot_nt(q, tile(k_ref, j, hs)) * scale, -jnp.inf)

        return (lambda: jnp.where(col <= row, _dot_nt(q, tile(k_ref, own, hs)) * scale, -jnp.inf),
                past_scores,
                lambda j: tile(v_ref, j, hs))

    res = _two_pass_softmax_pv(own, [stream(h) for h in range(HEADS)], blk, blk)
    for h, (acc, l) in enumerate(res):
        o_ref[0, :, h * HEAD_DIM:(h + 1) * HEAD_DIM] = (acc / l).astype(o_ref.dtype)


def _moba_attention(qkv, col0):
    bsz, s, _ = qkv.shape
    n_blocks = s // MOBA_BLOCK
    return pl.pallas_call(
        functools.partial(_moba_kernel, n_blocks=n_blocks),
        grid=(bsz, n_blocks),
        in_specs=[pl.BlockSpec((1, MOBA_BLOCK, MIX_W), lambda b, i: (b, i, col0)),
                  pl.BlockSpec((1, s, MIX_W), lambda b, i: (b, 0, col0 + 1)),
                  pl.BlockSpec((1, s, MIX_W), lambda b, i: (b, 0, col0 + 2))],
        out_specs=pl.BlockSpec((1, MOBA_BLOCK, MIX_W), lambda b, i: (b, i, 0)),
        out_shape=jax.ShapeDtypeStruct((bsz, s, MIX_W), BF16),
        scratch_shapes=[pltpu.VMEM((HEADS, LANES, HEAD_DIM), F32)],
        compiler_params=_cparams("parallel", "arbitrary"),
        name="moba_attention",
    )(qkv, qkv, qkv)


def _causal_conv(x, halo, first, w, xe_ref):
    ts = x.shape[0]
    xe_ref[0:CONV_HALO, :] = jnp.where(first, jnp.zeros_like(halo), halo)
    xe_ref[CONV_HALO:CONV_HALO + ts, :] = x
    out = w[CONV_K - 1:CONV_K] * x
    for k in range(CONV_K - 1):
        off = CONV_HALO - (CONV_K - 1) + k
        out = out + w[k:k + 1] * xe_ref[off:off + ts, :]
    return out


def _rglru_kernel(x_ref, xh_ref, gb_ref, cw_ref, cb_ref, wbd_ref, bbd_ref, lam_ref, o_ref,
                  xe_ref, h_ref, *, ts):
    i = pl.program_id(1)
    width = x_ref.shape[-1]
    xc = _causal_conv(x_ref[0], xh_ref[0], i == 0, cw_ref[...], xe_ref) + cb_ref[...]
    ri = jax.nn.sigmoid(_dot(xc.astype(BF16), wbd_ref[...]) + bbd_ref[...])
    r = ri[:, :width]
    ig = ri[:, width:]
    log_a = -LRU_C * r * _softplus(-lam_ref[...])
    a = jnp.exp(log_a)
    th = jnp.tanh(log_a)
    u = jnp.sqrt(-2.0 * th / (1.0 - th)) * (ig * xc)

    rows = lax.broadcasted_iota(jnp.int32, (ts, width), 0)
    d = 1
    while d < ts:
        keep = rows >= d
        a_sh = jnp.where(keep, pltpu.roll(a, d, axis=0), 1.0)
        u_sh = jnp.where(keep, pltpu.roll(u, d, axis=0), 0.0)
        u = a * u_sh + u
        a = a * a_sh
        d *= 2

    @pl.when(i == 0)
    def _():
        h_ref[...] = jnp.zeros_like(h_ref)

    hs = u + a * h_ref[0:1, :]
    h_ref[0:1, :] = hs[ts - 1:ts, :]
    o_ref[0] = (hs * jax.nn.gelu(gb_ref[0])).astype(o_ref.dtype)


def _rglru(cd, col0, conv_w, conv_b, wbd, bbd, lam, *, ts=256):
    bsz, s, _ = cd.shape
    w = MIX_W
    hb = ts // CONV_HALO
    const = lambda b, i: (0, 0)
    return pl.pallas_call(
        functools.partial(_rglru_kernel, ts=ts),
        grid=(bsz, s // ts),
        in_specs=[pl.BlockSpec((1, ts, w), lambda b, i: (b, i, col0)),
                  pl.BlockSpec((1, CONV_HALO, w), lambda b, i: (b, jnp.maximum(i * hb - 1, 0), col0)),
                  pl.BlockSpec((1, ts, w), lambda b, i: (b, i, col0 + 1)),
                  pl.BlockSpec((CONV_K, w), const),
                  pl.BlockSpec((1, w), const),
                  pl.BlockSpec((w, 2 * w), const),
                  pl.BlockSpec((1, 2 * w), const),
                  pl.BlockSpec((1, w), const)],
        out_specs=pl.BlockSpec((1, ts, w), lambda b, i: (b, i, 0)),
        out_shape=jax.ShapeDtypeStruct((bsz, s, w), BF16),
        scratch_shapes=[pltpu.VMEM((ts + CONV_HALO, w), F32), pltpu.VMEM((8, w), F32)],
        compiler_params=_cparams("parallel", "arbitrary"),
        name="rglru",
    )(cd, cd, cd, conv_w, conv_b, wbd, bbd, lam)


def _unit_lower_inverses(ms, ts):
    row = lax.broadcasted_iota(jnp.int32, (ts, ts), 0)
    col = lax.broadcasted_iota(jnp.int32, (ts, ts), 1)
    level = jnp.where(col < row, 31 - lax.clz(row ^ col), -1)
    xs = [jnp.where(row == col, 1.0, jnp.where(level == 0, -m, 0.0)) for m in ms]
    for lv in range(1, ts.bit_length() - 1):
        xb = [x.astype(BF16) for x in xs]
        cx = [_dot(jnp.where(level == lv, m, 0.0).astype(BF16), b).astype(BF16) for m, b in zip(ms, xb)]
        xs = [x - _dot(b, c) for x, b, c in zip(xs, xb, cx)]
    return xs


def _delta_prep_kernel(q_ref, qh_ref, k_ref, kh_ref, v_ref, vh_ref, sc_ref, sr_ref, cw_ref,
                       hc_ref, hr_ref,
                       qd_ref, kd_ref, u_ref, w_ref, qk_ref, egl_ref, xe_ref, *, ts):
    first = pl.program_id(1) == 0
    cw = cw_ref[...]
    q = _silu(_causal_conv(q_ref[0], qh_ref[0], first, cw[:, 0:MIX_W], xe_ref))
    k = _silu(_causal_conv(k_ref[0], kh_ref[0], first, cw[:, MIX_W:2 * MIX_W], xe_ref))
    v = _silu(_causal_conv(v_ref[0], vh_ref[0], first, cw[:, 2 * MIX_W:3 * MIX_W], xe_ref))

    sc = sc_ref[0]
    hc = hc_ref[...]
    beta_c = jax.nn.sigmoid(sc)
    g_c = -hc[0:1] * _softplus(sc + hc[1:2])
    sr = sr_ref[0]
    hr = hr_ref[...]
    g_r = -hr[:, 0:1] * _softplus(sr + hr[:, 1:2])

    row = lax.broadcasted_iota(jnp.int32, (ts, ts), 0)
    col = lax.broadcasted_iota(jnp.int32, (ts, ts), 1)
    lower_incl = jnp.where(col <= row, 1.0, 0.0).astype(BF16)
    gc_c = sum(_dot(lower_incl, p) for p in _split3_bf16(g_c))
    gc_r = sum(_dot_nt(p, lower_incl) for p in _split3_bf16(g_r))

    egl_ref[...] = jnp.zeros_like(egl_ref)
    heads = []
    for h in range(HEADS):
        sl = slice(h * HEAD_DIM, (h + 1) * HEAD_DIM)
        qh = q[:, sl]
        kh = k[:, sl]
        qh = qh * lax.rsqrt(jnp.sum(qh * qh, axis=-1, keepdims=True) + 1e-6) * HEAD_DIM ** -0.5
        kh = kh * lax.rsqrt(jnp.sum(kh * kh, axis=-1, keepdims=True) + 1e-6)
        beta = beta_c[:, h:h + 1]
        gcol = gc_c[:, HEADS + h:HEADS + h + 1]
        grow = gc_r[HEADS + h:HEADS + h + 1, :]
        glast = gcol[ts - 1:ts, :]
        decay = jnp.exp(jnp.where(col <= row, gcol - grow, -jnp.inf))
        kb = kh * beta
        khb = kh.astype(BF16)
        eg = jnp.exp(gcol)
        m = _dot_nt(kb.astype(BF16), khb) * decay
        qd_ref[0, :, sl] = (qh * eg).astype(qd_ref.dtype)
        kd_ref[0, :, sl] = (kh * jnp.exp(glast - gcol)).astype(kd_ref.dtype)
        qk_ref[0, h] = (_dot_nt(qh.astype(BF16), khb) * decay).astype(qk_ref.dtype)
        egl_ref[0, 0, h:h + 1, :] = jnp.broadcast_to(jnp.exp(glast), (1, LANES))
        heads.append((sl, m, (v[:, sl] * beta).astype(BF16), (kb * eg).astype(BF16)))

    xs = _unit_lower_inverses([hd[1] for hd in heads], ts)
    for (sl, _, vb, kbe), x in zip(heads, xs):
        xb = x.astype(BF16)
        u_ref[0, :, sl] = _dot(xb, vb)
        w_ref[0, :, sl] = _dot(xb, kbe).astype(w_ref.dtype)


def _delta_scan_kernel(qd_ref, kd_ref, u_ref, w_ref, qk_ref, egl_ref, z_ref, g_ref, o_ref, st_ref):
    @pl.when(pl.program_id(1) == 0)
    def _():
        st_ref[...] = jnp.zeros_like(st_ref)

    for h in range(HEADS):
        sl = slice(h * HEAD_DIM, (h + 1) * HEAD_DIM)
        state = st_ref[h]
        sb = state.astype(BF16)
        v_new = u_ref[0, :, sl] - _dot(w_ref[0, :, sl], sb)
        vb = v_new.astype(BF16)
        o = _dot(qd_ref[0, :, sl], sb) + _dot(qk_ref[0, h], vb)
        st_ref[h] = state * egl_ref[0, 0, h:h + 1, 0:1] + _dot_tn(kd_ref[0, :, sl], vb)
        o = o * lax.rsqrt(jnp.mean(o * o, axis=-1, keepdims=True) + EPS) * g_ref[...]
        o_ref[0, :, sl] = (o * _silu(z_ref[0, :, sl])).astype(o_ref.dtype)


def _gated_deltanet(cd, col0, small, conv_w, a_log, dt_bias, norm_g, *, ts=256):
    bsz, s, _ = cd.shape
    w = MIX_W
    nc = s // ts
    hb = ts // CONV_HALO
    small_rows = jnp.swapaxes(small[:, :, :2 * HEADS], 1, 2)
    ea = jnp.exp(a_log.astype(F32))
    head_cols = jnp.zeros((2, LANES), F32)
    head_cols = head_cols.at[0, HEADS:2 * HEADS].set(ea).at[1, HEADS:2 * HEADS].set(dt_bias)
    head_rows = jnp.zeros((2 * HEADS, 2), F32)
    head_rows = head_rows.at[HEADS:, 0].set(ea).at[HEADS:, 1].set(dt_bias)

    def tile_spec(c):
        return pl.BlockSpec((1, ts, w), lambda b, i: (b, i, col0 + c))

    def halo_spec(c):
        return pl.BlockSpec((1, CONV_HALO, w), lambda b, i: (b, jnp.maximum(i * hb - 1, 0), col0 + c))

    const = lambda b, i: (0, 0)
    row_out = pl.BlockSpec((1, ts, w), lambda b, i: (b, i, 0))
    qd, kd, u, wm, qk, egl = pl.pallas_call(
        functools.partial(_delta_prep_kernel, ts=ts),
        grid=(bsz, nc),
        in_specs=[tile_spec(0), halo_spec(0), tile_spec(1), halo_spec(1), tile_spec(2), halo_spec(2),
                  pl.BlockSpec((1, ts, LANES), lambda b, i: (b, i, 0)),
                  pl.BlockSpec((1, 2 * HEADS, ts), lambda b, i: (b, 0, i)),
                  pl.BlockSpec((CONV_K, 3 * w), const),
                  pl.BlockSpec((2, LANES), const),
                  pl.BlockSpec((2 * HEADS, 2), const)],
        out_specs=[row_out, row_out, row_out, row_out,
                   pl.BlockSpec((1, HEADS, ts, ts), lambda b, i: (b, 0, i, 0)),
                   pl.BlockSpec((1, 1, 8, LANES), lambda b, i: (b, i, 0, 0))],
        out_shape=[jax.ShapeDtypeStruct((bsz, s, w), BF16),
                   jax.ShapeDtypeStruct((bsz, s, w), BF16),
                   jax.ShapeDtypeStruct((bsz, s, w), F32),
                   jax.ShapeDtypeStruct((bsz, s, w), BF16),
                   jax.ShapeDtypeStruct((bsz, HEADS, s, ts), BF16),
                   jax.ShapeDtypeStruct((bsz, nc, 8, LANES), F32)],
        scratch_shapes=[pltpu.VMEM((ts + CONV_HALO, w), F32)],
        compiler_params=_cparams("parallel", "parallel"),
        name="delta_prep",
    )(cd, cd, cd, cd, cd, cd, small, small_rows, conv_w, head_cols, head_rows)

    return pl.pallas_call(
        _delta_scan_kernel,
        grid=(bsz, nc),
        in_specs=[row_out, row_out, row_out, row_out,
                  pl.BlockSpec((1, HEADS, ts, ts), lambda b, i: (b, 0, i, 0)),
                  pl.BlockSpec((1, 1, 8, LANES), lambda b, i: (b, i, 0, 0)),
                  tile_spec(3),
                  pl.BlockSpec((1, HEAD_DIM), const)],
        out_specs=row_out,
        out_shape=jax.ShapeDtypeStruct((bsz, s, w), BF16),
        scratch_shapes=[pltpu.VMEM((HEADS, HEAD_DIM, HEAD_DIM), F32)],
        compiler_params=_cparams("parallel", "arbitrary"),
        name="delta_scan",
    )(qd, kd, u, wm, qk, egl, cd, norm_g)


def _top2_route(logits):
    lane = lax.broadcasted_iota(jnp.int32, logits.shape, 1)
    big = jnp.int32(LANES)
    v1 = jnp.max(logits, axis=-1, keepdims=True)
    i1 = jnp.min(jnp.where(logits == v1, lane, big), axis=-1, keepdims=True)
    rest = jnp.where(lane == i1, -jnp.inf, logits)
    v2 = jnp.max(rest, axis=-1, keepdims=True)
    i2 = jnp.min(jnp.where(rest == v2, lane, big), axis=-1, keepdims=True)
    e2 = jnp.exp(v2 - v1)
    p1 = 1.0 / (1.0 + e2)
    p2 = e2 / (1.0 + e2)
    out = jnp.where(lane == 0, i1.astype(F32), jnp.where(lane == 1, i2.astype(F32), 0.0))
    return jnp.where(lane == 2, p1, jnp.where(lane == 3, p2, out))


def _merge_kernel(oa_ref, ob_ref, oc_ref, od_ref, gate_ref, h_ref, wb_ref, wo_ref, g_ref, *rest, route):
    if route:
        wr_ref, h_out_ref, xn_ref, route_ref = rest
    else:
        h_out_ref, xn_ref = rest
    d = h_ref.shape[-1]
    y = None
    for n, o_ref in enumerate((oa_ref, ob_ref, oc_ref, od_ref)):
        yb = _dot(o_ref[...], wb_ref[n]) * gate_ref[:, n * d:(n + 1) * d].astype(F32)
        y = yb if y is None else y + yb
    h = h_ref[...] + _dot(y.astype(BF16), wo_ref[...])
    h_out_ref[...] = h
    xn = h * lax.rsqrt(jnp.mean(h * h, axis=-1, keepdims=True) + EPS) * g_ref[...]
    xn_ref[...] = xn.astype(xn_ref.dtype)
    if route:
        lane = lax.broadcasted_iota(jnp.int32, route_ref.shape, 1)
        logits = jnp.where(lane < N_EXPERTS, _dot_3pass(xn, wr_ref[...]), -jnp.inf)
        route_ref[...] = _top2_route(logits)


def _merge(branches, gates, h, wb, wo, g, wr, *, tm=512):
    t, d = h.shape
    w = MIX_W
    route = wr is not None
    row = lambda i: (i, 0)
    const2 = lambda i: (0, 0)
    in_specs = [pl.BlockSpec((tm, w), row)] * 4 + [
        pl.BlockSpec((tm, 4 * d), row),
        pl.BlockSpec((tm, d), row),
        pl.BlockSpec((4, w, d), lambda i: (0, 0, 0)),
        pl.BlockSpec((d, d), const2),
        pl.BlockSpec((1, d), const2)]
    out_specs = [pl.BlockSpec((tm, d), row), pl.BlockSpec((tm, d), row)]
    out_shape = [jax.ShapeDtypeStruct((t, d), F32), jax.ShapeDtypeStruct((t, d), F32 if route else BF16)]
    args = [*branches, gates, h, wb, wo, g]
    if route:
        in_specs.append(pl.BlockSpec((d, LANES), const2))
        out_specs.append(pl.BlockSpec((tm, LANES), row))
        out_shape.append(jax.ShapeDtypeStruct((t, LANES), F32))
        args.append(wr)
    return pl.pallas_call(
        functools.partial(_merge_kernel, route=route),
        grid=(t // tm,),
        in_specs=in_specs, out_specs=out_specs, out_shape=out_shape,
        compiler_params=_cparams("parallel"),
        name="merge",
    )(*args)


def _rms(h, g):
    return h * lax.rsqrt(jnp.mean(h * h, axis=-1, keepdims=True) + EPS) * g


def _ffn_kernel(x_ref, h_ref, w1_ref, w3_ref, w2_ref, gf_ref, o_ref, acc_ref, *, final_norm):
    f = pl.program_id(1)

    @pl.when(f == 0)
    def _():
        acc_ref[...] = h_ref[...]

    x = x_ref[...]
    a = _silu(_dot(x, w1_ref[...])) * _dot(x, w3_ref[...])
    acc_ref[...] += _dot(a.astype(BF16), w2_ref[...])

    @pl.when(f == pl.num_programs(1) - 1)
    def _():
        h = acc_ref[...]
        o_ref[...] = _rms(h, gf_ref[...]) if final_norm else h


def _ffn(xn, h, w1, w3, w2, g_final, *, final_norm, tm, tf):
    t, d = h.shape
    ff = w1.shape[1]
    row = lambda i, f: (i, 0)
    return pl.pallas_call(
        functools.partial(_ffn_kernel, final_norm=final_norm),
        grid=(t // tm, ff // tf),
        in_specs=[pl.BlockSpec((tm, d), row),
                  pl.BlockSpec((tm, d), row),
                  pl.BlockSpec((d, tf), lambda i, f: (0, f)),
                  pl.BlockSpec((d, tf), lambda i, f: (0, f)),
                  pl.BlockSpec((tf, d), lambda i, f: (f, 0)),
                  pl.BlockSpec((1, d), lambda i, f: (0, 0))],
        out_specs=pl.BlockSpec((tm, d), row),
        out_shape=jax.ShapeDtypeStruct((t, d), F32),
        scratch_shapes=[pltpu.VMEM((tm, d), F32)],
        compiler_params=_cparams("parallel", "arbitrary"),
        name="ffn",
    )(xn, h, w1, w3, w2, g_final)


def _row_copy(src_hbm, dst_ref, src_row, dst_row, sem):
    return pltpu.make_async_copy(src_hbm.at[pl.ds(src_row, 1)], dst_ref.at[pl.ds(dst_row, 1)], sem)


def _start_row_gather(idx_ref, idx_base, idx_stride, src_hbm, dst_ref, sem, n_rows):
    def issue(r, c):
        _row_copy(src_hbm, dst_ref, idx_ref[idx_base + r * idx_stride], r, sem).start()
        return c

    lax.fori_loop(0, n_rows, issue, 0, unroll=8)


def _wait_row_gather(src_hbm, dst_ref, sem, n_rows):
    def wait(r, c):
        _row_copy(src_hbm, dst_ref, 0, r, sem).wait()
        return c

    lax.fori_loop(0, n_rows, wait, 0, unroll=8)


def _gather_kernel(idx_ref, src_hbm, o_ref, sem, *, tg):
    _start_row_gather(idx_ref, pl.program_id(0) * tg, 1, src_hbm, o_ref, sem, tg)
    _wait_row_gather(src_hbm, o_ref, sem, tg)


def _gather_rows(src, idx, *, tg):
    n = idx.shape[0]
    d = src.shape[1]
    return pl.pallas_call(
        functools.partial(_gather_kernel, tg=tg),
        grid_spec=pltpu.PrefetchScalarGridSpec(
            num_scalar_prefetch=1,
            grid=(n // tg,),
            in_specs=[pl.BlockSpec(memory_space=pl.ANY)],
            out_specs=pl.BlockSpec((tg, d), lambda i, idx: (i, 0)),
            scratch_shapes=[pltpu.SemaphoreType.DMA(())]),
        out_shape=jax.ShapeDtypeStruct((n, d), src.dtype),
        compiler_params=_cparams("arbitrary"),
        name="gather_rows",
    )(idx, src)


def _expert_ffn_kernel(te_ref, nu_ref, x_ref, w1_ref, w3_ref, w2_ref, o_ref, xb_ref, acc_ref):
    i = pl.program_id(0)
    f = pl.program_id(1)
    used = i < nu_ref[0]

    @pl.when(used & (f == 0))
    def _():
        xb_ref[...] = x_ref[...].astype(BF16)

    @pl.when(used)
    def _():
        x = xb_ref[...]
        a = _silu(_dot(x, w1_ref[0])) * _dot(x, w3_ref[0])
        y = _dot(a.astype(BF16), w2_ref[0])

        @pl.when(f == 0)
        def _():
            acc_ref[...] = y

        @pl.when(f > 0)
        def _():
            acc_ref[...] += y

    last = f == pl.num_programs(1) - 1

    @pl.when(used & last)
    def _():
        o_ref[...] = acc_ref[...]

    @pl.when(jnp.logical_not(used) & last)
    def _():
        o_ref[...] = jnp.zeros_like(o_ref)


def _expert_ffn(x_sorted, tile_expert, n_used, w1, w3, w2, *, tm, tf):
    tp, d = x_sorted.shape
    ff = w1.shape[2]
    return pl.pallas_call(
        _expert_ffn_kernel,
        grid_spec=pltpu.PrefetchScalarGridSpec(
            num_scalar_prefetch=2,
            grid=(tp // tm, ff // tf),
            in_specs=[pl.BlockSpec((tm, d), lambda i, f, te, nu: (i, 0)),
                      pl.BlockSpec((1, d, tf), lambda i, f, te, nu: (te[i], 0, f)),
                      pl.BlockSpec((1, d, tf), lambda i, f, te, nu: (te[i], 0, f)),
                      pl.BlockSpec((1, tf, d), lambda i, f, te, nu: (te[i], f, 0))],
            out_specs=pl.BlockSpec((tm, d), lambda i, f, te, nu: (i, 0)),
            scratch_shapes=[pltpu.VMEM((tm, d), BF16), pltpu.VMEM((tm, d), F32)]),
        out_shape=jax.ShapeDtypeStruct((tp, d), F32),
        compiler_params=_cparams("parallel", "arbitrary"),
        name="expert_ffn",
    )(tile_expert, n_used, x_sorted, w1, w3, w2)


def _combine_kernel(pos_ref, y_hbm, h_ref, route_ref, gf_ref, o_ref, y1_ref, y2_ref, sem, *, tc, final_norm):
    base = pl.program_id(0) * tc * 2
    _start_row_gather(pos_ref, base, 2, y_hbm, y1_ref, sem.at[0], tc)
    _start_row_gather(pos_ref, base + 1, 2, y_hbm, y2_ref, sem.at[1], tc)
    _wait_row_gather(y_hbm, y1_ref, sem.at[0], tc)
    _wait_row_gather(y_hbm, y2_ref, sem.at[1], tc)
    route = route_ref[...]
    h = h_ref[...] + route[:, 2:3] * y1_ref[...] + route[:, 3:4] * y2_ref[...]
    o_ref[...] = _rms(h, gf_ref[...]) if final_norm else h


def _combine(y_sorted, pos, h, route, g_final, *, final_norm, tc):
    t, d = h.shape
    return pl.pallas_call(
        functools.partial(_combine_kernel, tc=tc, final_norm=final_norm),
        grid_spec=pltpu.PrefetchScalarGridSpec(
            num_scalar_prefetch=1,
            grid=(t // tc,),
            in_specs=[pl.BlockSpec(memory_space=pl.ANY),
                      pl.BlockSpec((tc, d), lambda i, pos: (i, 0)),
                      pl.BlockSpec((tc, LANES), lambda i, pos: (i, 0)),
                      pl.BlockSpec((1, d), lambda i, pos: (0, 0))],
            out_specs=pl.BlockSpec((tc, d), lambda i, pos: (i, 0)),
            scratch_shapes=[pltpu.VMEM((tc, d), F32), pltpu.VMEM((tc, d), F32),
                            pltpu.SemaphoreType.DMA((2,))]),
        out_shape=jax.ShapeDtypeStruct((t, d), F32),
        compiler_params=_cparams("arbitrary"),
        name="combine",
    )(pos, y_sorted, h, route, g_final)


def _dispatch_plan(route, tm):
    t = route.shape[0]
    e_flat = route[:, :2].astype(jnp.int32).reshape(-1)
    onehot = (e_flat[:, None] == jnp.arange(N_EXPERTS, dtype=jnp.int32)[None, :]).astype(jnp.int32)
    csum = jnp.cumsum(onehot, axis=0)
    rank = jnp.sum((csum - onehot) * onehot, axis=1)
    counts = csum[-1]
    padded = ((counts + tm - 1) // tm) * tm
    ends = jnp.cumsum(padded)
    pos = ((ends - padded)[e_flat] + rank).astype(jnp.int32)
    n_rows = 2 * t + N_EXPERTS * tm
    src_token = jnp.zeros((n_rows,), jnp.int32).at[pos].set(jnp.arange(2 * t, dtype=jnp.int32) // 2)
    tile_start = jnp.arange(n_rows // tm, dtype=jnp.int32) * tm
    tile_expert = jnp.minimum(jnp.searchsorted(ends, tile_start, side="right"), N_EXPERTS - 1)
    n_used = (ends[-1:] // tm).astype(jnp.int32)
    return src_token, pos, tile_expert.astype(jnp.int32), n_used


def _moe(xn, h, route, w1, w3, w2, g_final, *, final_norm, tm=512, tf=512):
    src_token, pos, tile_expert, n_used = _dispatch_plan(route, tm)
    x_sorted = _gather_rows(xn, src_token, tg=tm)
    y_sorted = _expert_ffn(x_sorted, tile_expert, n_used, w1, w3, w2, tm=tm, tf=tf)
    return _combine(y_sorted, pos, h, route, g_final, final_norm=final_norm, tc=512)


def _block_diag(w):
    g, di, dj = w.shape
    eye = jnp.eye(g, dtype=w.dtype)
    return (eye[:, None, :, None] * w[:, :, None, :]).reshape(g * di, g * dj)


def kernel(x, norm_mix, norm_ffn, norm_final, w_in, lam_q1, lam_k1, lam_q2, lam_k2, a_subln, c_conv_w, c_conv_b, c_w_a, c_b_a, c_w_x, c_b_x, c_lambda, d_conv_w, d_A_log, d_dt_bias, d_norm, w_gate, b_gate, w_branch, w_out, ffn_w1, ffn_w3, ffn_w2, router, moe_w1, moe_w3, moe_w2):
    bsz, s, d = x.shape
    t = bsz * s
    depth = w_in.shape[0]
    n_ab = 6 * MIX_W
    n_cd = 6 * MIX_W
    h = x.reshape(t, d)
    for l in range(depth):
        g_mix = norm_mix[l].reshape(1, d)
        w_l = w_in[l]
        w_ab = w_l[:, :n_ab].astype(BF16)
        w_cd = w_l[:, n_ab:n_ab + n_cd].astype(BF16)
        w_small = jnp.pad(w_l[:, n_ab + n_cd:], ((0, 0), (0, LANES - 2 * HEADS)))
        ab = _norm_proj(h, g_mix, w_ab, jnp.zeros((1, n_ab), F32), out_dtype=BF16, tm=512, tn=1536)
        cd = _norm_proj(h, g_mix, w_cd, jnp.zeros((1, n_cd), F32), out_dtype=F32, tm=512, tn=1536)
        small = _norm_proj(h, g_mix, w_small, jnp.zeros((1, LANES), F32), out_dtype=F32, tm=512,
                           tn=LANES, precise=True)
        gates = _norm_proj(h, g_mix, w_gate[l].reshape(d, 4 * d).astype(BF16),
                           b_gate[l].reshape(1, 4 * d), out_dtype=BF16, tm=512, tn=1024, act="sigmoid")

        ab3 = ab.reshape(bsz, s, n_ab)
        cd3 = cd.reshape(bsz, s, n_cd)
        lam_vecs = jnp.stack([lam_q1[l], lam_k1[l], lam_q2[l], lam_k2[l]])
        o_a = _diff_attention(ab3, 0, lam_vecs, a_subln[l].reshape(1, HEAD_DIM), l)
        o_b = _moba_attention(ab3, 3)
        wbd = jnp.concatenate([_block_diag(c_w_a[l]), _block_diag(c_w_x[l])], axis=1).astype(BF16)
        bbd = jnp.concatenate([c_b_a[l].reshape(1, MIX_W), c_b_x[l].reshape(1, MIX_W)], axis=1)
        o_c = _rglru(cd3, 0, c_conv_w[l], c_conv_b[l].reshape(1, MIX_W), wbd, bbd,
                     c_lambda[l].reshape(1, MIX_W))
        o_d = _gated_deltanet(cd3, 2, small.reshape(bsz, s, LANES), d_conv_w[l], d_A_log[l],
                              d_dt_bias[l], d_norm[l].reshape(1, HEAD_DIM))

        moe_layer = l % 2 == 1
        wr = jnp.pad(router[l // 2], ((0, 0), (0, LANES - N_EXPERTS))) if moe_layer else None
        branches = [o.reshape(t, MIX_W) for o in (o_a, o_b, o_c, o_d)]
        merged = _merge(branches, gates, h, w_branch[l].astype(BF16), w_out[l].astype(BF16),
                        norm_ffn[l].reshape(1, d), wr)
        last = l == depth - 1
        g_final = norm_final.reshape(1, d)
        if moe_layer:
            h, xn, route = merged
            h = _moe(xn, h, route, moe_w1[l // 2].astype(BF16), moe_w3[l // 2].astype(BF16),
                     moe_w2[l // 2].astype(BF16), g_final, final_norm=last)
        else:
            h, xn = merged
            h = _ffn(xn, h, ffn_w1[l // 2].astype(BF16), ffn_w3[l // 2].astype(BF16),
                     ffn_w2[l // 2].astype(BF16), g_final, final_norm=last, tm=512, tf=1408)
    return h.reshape(bsz, s, d)
```

```python
import functools
import math

import jax
import jax.numpy as jnp
from jax import lax
from jax.experimental import pallas as pl
from jax.experimental.pallas import tpu as pltpu

F32 = jnp.float32
BF16 = jnp.bfloat16

EPS = 1e-6
MIX_W = 512
HEADS = 4
HEAD_DIM = MIX_W // HEADS
A_QK_DIM = HEAD_DIM // 2
MOBA_BLOCK = 256
MOBA_TOPK = 3
CONV_K = 4
LRU_C = 8.0
N_EXPERTS = 8
LANES = 128
CONV_HALO = 8
VMEM_LIMIT = 56 * 1024 * 1024


def _cparams(*sem):
    return pltpu.CompilerParams(dimension_semantics=sem, vmem_limit_bytes=VMEM_LIMIT)


def _dot(a, b):
    return jnp.dot(a, b, preferred_element_type=F32)


def _dot_nt(a, b):
    return lax.dot_general(a, b, (((1,), (1,)), ((), ())), preferred_element_type=F32)


def _dot_tn(a, b):
    return lax.dot_general(a, b, (((0,), (0,)), ((), ())), preferred_element_type=F32)


def _split_bf16(a):
    hi = a.astype(BF16)
    return hi, (a - hi.astype(F32)).astype(BF16)


def _split3_bf16(a):
    parts = []
    for _ in range(3):
        p = a.astype(BF16)
        parts.append(p)
        a = a - p.astype(F32)
    return parts


def _dot_3pass(a, b):
    ah, al = _split_bf16(a)
    bh, bl = _split_bf16(b)
    return _dot(ah, bh) + (_dot(ah, bl) + _dot(al, bh))


def _silu(x):
    return x * jax.nn.sigmoid(x)


def _softplus(x):
    return jnp.maximum(x, 0.0) + jnp.log1p(jnp.exp(-jnp.abs(x)))


def _in_proj_kernel(x_ref, g_ref, wab_ref, wcd_ref, wg_ref, bg_ref, ws_ref,
                    ab_ref, cd_ref, gate_ref, small_ref, *, tn):
    x = x_ref[...]
    xn = x * lax.rsqrt(jnp.mean(x * x, axis=-1, keepdims=True) + EPS) * g_ref[...]
    xb = xn.astype(BF16)
    for c in range(0, ab_ref.shape[1], tn):
        ab_ref[:, c:c + tn] = _dot(xb, wab_ref[:, c:c + tn]).astype(ab_ref.dtype)
    for c in range(0, cd_ref.shape[1], tn):
        cd_ref[:, c:c + tn] = _dot(xb, wcd_ref[:, c:c + tn]).astype(cd_ref.dtype)
    for c in range(0, gate_ref.shape[1], tn):
        y = _dot(xb, wg_ref[:, c:c + tn]) + bg_ref[:, c:c + tn]
        gate_ref[:, c:c + tn] = jax.nn.sigmoid(y).astype(gate_ref.dtype)
    small_ref[...] = _dot_3pass(xn, ws_ref[...])


def _in_proj(x, g, w_ab, w_cd, w_gate, b_gate, w_small, *, tm=512, tn=1024):
    t, d = x.shape
    row = lambda i: (i, 0)
    const = lambda i: (0, 0)

    def resident(a):
        return pl.BlockSpec(a.shape, const, pipeline_mode=pl.Buffered(1))

    outs = [(w_ab.shape[1], BF16), (w_cd.shape[1], F32), (w_gate.shape[1], BF16), (w_small.shape[1], F32)]
    return pl.pallas_call(
        functools.partial(_in_proj_kernel, tn=tn),
        grid=(t // tm,),
        in_specs=[pl.BlockSpec((tm, d), row), resident(g), resident(w_ab), resident(w_cd),
                  resident(w_gate), resident(b_gate), resident(w_small)],
        out_specs=[pl.BlockSpec((tm, n), row) for n, _ in outs],
        out_shape=[jax.ShapeDtypeStruct((t, n), dt) for n, dt in outs],
        compiler_params=_cparams("parallel"),
        name="in_proj",
    )(x, g, w_ab, w_cd, w_gate, b_gate, w_small)


def _two_pass_softmax_pv(n_past, streams, s_ref, tq, tk):
    chunks = [slice(c * LANES, (c + 1) * LANES) for c in range(tk // LANES)]

    def lane_fold(s, op):
        out = s[:, chunks[0]]
        for c in chunks[1:]:
            out = op(out, s[:, c])
        return out

    s_diag = [st[0]() for st in streams]
    mruns = tuple(lane_fold(s, jnp.maximum) for s in s_diag)

    def pass1(j, mr):
        out = []
        for n, (m, st) in enumerate(zip(mr, streams)):
            s = st[1](j)
            s_ref[n, j] = s
            out.append(jnp.maximum(m, lane_fold(s, jnp.maximum)))
        return tuple(out)

    mruns = lax.fori_loop(0, n_past, pass1, mruns)
    mbs = [jnp.broadcast_to(jnp.max(m, axis=-1, keepdims=True), (tq, LANES)) for m in mruns]

    def probs(s, mb):
        ps = [jnp.exp(s[:, c] - mb) for c in chunks]
        lsum = ps[0]
        for p in ps[1:]:
            lsum = lsum + p
        return jnp.concatenate(ps, axis=1).astype(BF16), lsum

    carry = []
    for s, mb, st in zip(s_diag, mbs, streams):
        p, lsum = probs(s, mb)
        carry.append((lsum, _dot(p, st[2](n_past))))

    def pass2(j, carry):
        out = []
        for n, ((lr, ac), mb, st) in enumerate(zip(carry, mbs, streams)):
            p, lsum = probs(s_ref[n, j], mb)
            out.append((lr + lsum, ac + _dot(p, st[2](j))))
        return tuple(out)

    carry = lax.fori_loop(0, n_past, pass2, tuple(carry))
    return [(ac, jnp.sum(lr, axis=-1, keepdims=True)) for lr, ac in carry]


def _diff_attn_kernel(q_ref, k_ref, v_ref, lam_ref, g_ref, o_ref, s_ref, *, tq, lam_init):
    i = pl.program_id(2)
    q = q_ref[0]
    lane = lax.broadcasted_iota(jnp.int32, q.shape, 1)
    scale = A_QK_DIM ** -0.5
    zero = jnp.zeros_like(q)
    row = lax.broadcasted_iota(jnp.int32, (tq, tq), 0)
    col = lax.broadcasted_iota(jnp.int32, (tq, tq), 1)
    causal = col <= row

    def tile(ref, j):
        return ref[0, pl.ds(pl.multiple_of(j * tq, tq), tq), :]

    def stream(c):
        in_half = (lane < A_QK_DIM) if c == 0 else (lane >= A_QK_DIM)
        qc = jnp.where(in_half, q, zero) * scale
        return (lambda: jnp.where(causal, _dot_nt(qc, tile(k_ref, i)), -jnp.inf),
                lambda j: _dot_nt(qc, tile(k_ref, j)),
                lambda j: tile(v_ref, j))

    (a0, l0), (a1, l1) = _two_pass_softmax_pv(i, [stream(0), stream(1)], s_ref, tq, tq)

    lv = lam_ref[...]
    lam = (jnp.exp(jnp.sum(lv[0:1] * lv[1:2], axis=-1, keepdims=True))
           - jnp.exp(jnp.sum(lv[2:3] * lv[3:4], axis=-1, keepdims=True)) + lam_init)
    o = a0 / l0 - lam * (a1 / l1)
    o = o * lax.rsqrt(jnp.mean(o * o, axis=-1, keepdims=True) + 1e-5) * g_ref[...]
    o_ref[0] = (o * (1.0 - lam_init)).astype(o_ref.dtype)


def _diff_attention(qkv, col0, lam_vecs, sub_g, layer_idx, *, tq=512):
    bsz, s, _ = qkv.shape
    lam_init = 0.8 - 0.6 * math.exp(-0.3 * layer_idx)
    return pl.pallas_call(
        functools.partial(_diff_attn_kernel, tq=tq, lam_init=lam_init),
        grid=(bsz, HEADS, s // tq),
        in_specs=[pl.BlockSpec((1, tq, HEAD_DIM), lambda b, h, i: (b, i, col0 + h)),
                  pl.BlockSpec((1, s, HEAD_DIM), lambda b, h, i: (b, 0, col0 + HEADS + h)),
                  pl.BlockSpec((1, s, HEAD_DIM), lambda b, h, i: (b, 0, col0 + 2 * HEADS + h)),
                  pl.BlockSpec((4, A_QK_DIM), lambda b, h, i: (0, 0)),
                  pl.BlockSpec((1, HEAD_DIM), lambda b, h, i: (0, 0))],
        out_specs=pl.BlockSpec((1, tq, HEAD_DIM), lambda b, h, i: (b, i, h)),
        out_shape=jax.ShapeDtypeStruct((bsz, s, MIX_W), BF16),
        scratch_shapes=[pltpu.VMEM((2, s // tq - 1, tq, tq), F32)],
        compiler_params=_cparams("parallel", "parallel", "arbitrary"),
        name="diff_attention",
    )(qkv, qkv, qkv, lam_vecs, sub_g)


def _moba_kernel(q_ref, k_ref, v_ref, o_ref, kmean_ref, s_ref, *, n_blocks):
    own = pl.program_id(1)
    blk = MOBA_BLOCK
    scale = HEAD_DIM ** -0.5
    nb8 = kmean_ref.shape[1]

    @pl.when(own == 0)
    def _():
        kmean_ref[...] = jnp.zeros_like(kmean_ref)
        for h in range(HEADS):
            kh = k_ref[0, :, h * HEAD_DIM:(h + 1) * HEAD_DIM].astype(F32)
            kmean_ref[h, 0:n_blocks, :] = jnp.mean(kh.reshape(n_blocks, blk, HEAD_DIM), axis=1)

    cand = lax.broadcasted_iota(jnp.int32, (nb8, blk), 0)
    valid = cand < own
    lane = lax.broadcasted_iota(jnp.int32, (blk, LANES), 1)
    row = lax.broadcasted_iota(jnp.int32, (blk, blk), 0)
    col = lax.broadcasted_iota(jnp.int32, (blk, blk), 1)
    topk = min(MOBA_TOPK, n_blocks - 1)

    def tile(ref, j, hs):
        return ref[0, pl.ds(pl.multiple_of(j * blk, blk), blk), hs]

    def stream(h):
        hs = slice(h * HEAD_DIM, (h + 1) * HEAD_DIM)
        q = q_ref[0, :, hs]
        km_hi, km_lo = _split_bf16(kmean_ref[h])
        gate = _dot_nt(km_hi, q) + _dot_nt(km_lo, q)
        gv = jnp.where(valid, gate, -jnp.inf)
        rank = jnp.zeros((nb8, blk), F32)
        for jp in range(n_blocks - 1):
            gj = gv[jp:jp + 1, :]
            ahead = (gj > gv) | ((gj == gv) & (jp < cand))
            rank = rank + jnp.where(ahead, 1.0, 0.0)
        sel_t = jnp.where(valid & (rank < float(topk)), 1.0, 0.0)
        sel = jnp.concatenate([sel_t, jnp.zeros((LANES - nb8, blk), F32)], axis=0).T

        def past_scores(j):
            sel_j = jnp.max(jnp.where(lane == j, sel, 0.0), axis=-1, keepdims=True)
            return jnp.where(sel_j > 0.0, _dot_nt(q, tile(k_ref, j, hs)) * scale, -jnp.inf)

        return (lambda: jnp.where(col <= row, _dot_nt(q, tile(k_ref, own, hs)) * scale, -jnp.inf),
                past_scores,
                lambda j: tile(v_ref, j, hs))

    res = _two_pass_softmax_pv(own, [stream(h) for h in range(HEADS)], s_ref, blk, blk)
    for h, (acc, l) in enumerate(res):
        o_ref[0, :, h * HEAD_DIM:(h + 1) * HEAD_DIM] = (acc / l).astype(o_ref.dtype)


def _moba_attention(qkv, col0):
    bsz, s, _ = qkv.shape
    n_blocks = s // MOBA_BLOCK
    return pl.pallas_call(
        functools.partial(_moba_kernel, n_blocks=n_blocks),
        grid=(bsz, n_blocks),
        in_specs=[pl.BlockSpec((1, MOBA_BLOCK, MIX_W), lambda b, i: (b, i, col0)),
                  pl.BlockSpec((1, s, MIX_W), lambda b, i: (b, 0, col0 + 1)),
                  pl.BlockSpec((1, s, MIX_W), lambda b, i: (b, 0, col0 + 2))],
        out_specs=pl.BlockSpec((1, MOBA_BLOCK, MIX_W), lambda b, i: (b, i, 0)),
        out_shape=jax.ShapeDtypeStruct((bsz, s, MIX_W), BF16),
        scratch_shapes=[pltpu.VMEM((HEADS, -(-n_blocks // 8) * 8, HEAD_DIM), F32),
                        pltpu.VMEM((HEADS, n_blocks - 1, MOBA_BLOCK, MOBA_BLOCK), F32)],
        compiler_params=_cparams("parallel", "arbitrary"),
        name="moba_attention",
    )(qkv, qkv, qkv)


def _causal_conv(x, halo, first, w, xe_ref):
    ts = x.shape[0]
    xe_ref[0:CONV_HALO, :] = jnp.where(first, jnp.zeros_like(halo), halo)
    xe_ref[CONV_HALO:CONV_HALO + ts, :] = x
    out = w[CONV_K - 1:CONV_K] * x
    for k in range(CONV_K - 1):
        off = CONV_HALO - (CONV_K - 1) + k
        out = out + w[k:k + 1] * xe_ref[off:off + ts, :]
    return out


def _rglru_kernel(x_ref, xh_ref, gb_ref, cw_ref, cb_ref, wbd_ref, bbd_ref, lam_ref, o_ref,
                  xe_ref, h_ref, *, ts):
    i = pl.program_id(1)
    width = x_ref.shape[-1]
    xc = _causal_conv(x_ref[0], xh_ref[0], i == 0, cw_ref[...], xe_ref) + cb_ref[...]
    ri = jax.nn.sigmoid(_dot(xc.astype(BF16), wbd_ref[...]) + bbd_ref[...])
    r = ri[:, :width]
    ig = ri[:, width:]
    log_a = -LRU_C * r * _softplus(-lam_ref[...])
    a = jnp.exp(log_a)
    th = jnp.tanh(log_a)
    u = jnp.sqrt(-2.0 * th / (1.0 - th)) * (ig * xc)

    rows = lax.broadcasted_iota(jnp.int32, (ts, width), 0)
    d = 1
    while d < ts:
        keep = rows >= d
        a_sh = jnp.where(keep, pltpu.roll(a, d, axis=0), 1.0)
        u_sh = jnp.where(keep, pltpu.roll(u, d, axis=0), 0.0)
        u = a * u_sh + u
        a = a * a_sh
        d *= 2

    @pl.when(i == 0)
    def _():
        h_ref[...] = jnp.zeros_like(h_ref)

    hs = u + a * h_ref[0:1, :]
    h_ref[0:1, :] = hs[ts - 1:ts, :]
    o_ref[0] = (hs * jax.nn.gelu(gb_ref[0])).astype(o_ref.dtype)


def _rglru(cd, col0, conv_w, conv_b, wbd, bbd, lam, *, ts=256):
    bsz, s, _ = cd.shape
    w = MIX_W
    hb = ts // CONV_HALO
    const = lambda b, i: (0, 0)
    return pl.pallas_call(
        functools.partial(_rglru_kernel, ts=ts),
        grid=(bsz, s // ts),
        in_specs=[pl.BlockSpec((1, ts, w), lambda b, i: (b, i, col0)),
                  pl.BlockSpec((1, CONV_HALO, w), lambda b, i: (b, jnp.maximum(i * hb - 1, 0), col0)),
                  pl.BlockSpec((1, ts, w), lambda b, i: (b, i, col0 + 1)),
                  pl.BlockSpec((CONV_K, w), const),
                  pl.BlockSpec((1, w), const),
                  pl.BlockSpec((w, 2 * w), const),
                  pl.BlockSpec((1, 2 * w), const),
                  pl.BlockSpec((1, w), const)],
        out_specs=pl.BlockSpec((1, ts, w), lambda b, i: (b, i, 0)),
        out_shape=jax.ShapeDtypeStruct((bsz, s, w), BF16),
        scratch_shapes=[pltpu.VMEM((ts + CONV_HALO, w), F32), pltpu.VMEM((8, w), F32)],
        compiler_params=_cparams("parallel", "arbitrary"),
        name="rglru",
    )(cd, cd, cd, conv_w, conv_b, wbd, bbd, lam)


def _unit_lower_inverses(ms, ts):
    row = lax.broadcasted_iota(jnp.int32, (ts, ts), 0)
    col = lax.broadcasted_iota(jnp.int32, (ts, ts), 1)
    level = jnp.where(col < row, 31 - lax.clz(row ^ col), -1)
    xs = [jnp.where(row == col, 1.0, jnp.where(level == 0, -m, 0.0)) for m in ms]
    for lv in range(1, ts.bit_length() - 1):
        xb = [x.astype(BF16) for x in xs]
        cx = [_dot(jnp.where(level == lv, m, 0.0).astype(BF16), b).astype(BF16) for m, b in zip(ms, xb)]
        xs = [x - _dot(b, c) for x, b, c in zip(xs, xb, cx)]
    return xs


def _delta_prep_kernel(q_ref, qh_ref, k_ref, kh_ref, v_ref, vh_ref, sc_ref, sr_ref, cw_ref,
                       hc_ref, hr_ref,
                       qd_ref, kd_ref, u_ref, w_ref, qk_ref, egl_ref, xe_ref, *, ts):
    first = pl.program_id(1) == 0
    cw = cw_ref[...]
    q = _silu(_causal_conv(q_ref[0], qh_ref[0], first, cw[:, 0:MIX_W], xe_ref))
    k = _silu(_causal_conv(k_ref[0], kh_ref[0], first, cw[:, MIX_W:2 * MIX_W], xe_ref))
    v = _silu(_causal_conv(v_ref[0], vh_ref[0], first, cw[:, 2 * MIX_W:3 * MIX_W], xe_ref))

    sc = sc_ref[0]
    hc = hc_ref[...]
    beta_c = jax.nn.sigmoid(sc)
    g_c = -hc[0:1] * _softplus(sc + hc[1:2])
    sr = sr_ref[0]
    hr = hr_ref[...]
    g_r = -hr[:, 0:1] * _softplus(sr + hr[:, 1:2])

    row = lax.broadcasted_iota(jnp.int32, (ts, ts), 0)
    col = lax.broadcasted_iota(jnp.int32, (ts, ts), 1)
    lower_incl = jnp.where(col <= row, 1.0, 0.0).astype(BF16)
    gc_c = sum(_dot(lower_incl, p) for p in _split3_bf16(g_c))
    gc_r = sum(_dot_nt(p, lower_incl) for p in _split3_bf16(g_r))

    egl_ref[...] = jnp.zeros_like(egl_ref)
    heads = []
    for h in range(HEADS):
        sl = slice(h * HEAD_DIM, (h + 1) * HEAD_DIM)
        qh = q[:, sl]
        kh = k[:, sl]
        qh = qh * lax.rsqrt(jnp.sum(qh * qh, axis=-1, keepdims=True) + 1e-6) * HEAD_DIM ** -0.5
        kh = kh * lax.rsqrt(jnp.sum(kh * kh, axis=-1, keepdims=True) + 1e-6)
        beta = beta_c[:, h:h + 1]
        gcol = gc_c[:, HEADS + h:HEADS + h + 1]
        grow = gc_r[HEADS + h:HEADS + h + 1, :]
        glast = gcol[ts - 1:ts, :]
        decay = jnp.exp(jnp.where(col <= row, gcol - grow, -jnp.inf))
        kb = kh * beta
        khb = kh.astype(BF16)
        eg = jnp.exp(gcol)
        m = _dot_nt(kb.astype(BF16), khb) * decay
        qd_ref[0, :, sl] = (qh * eg).astype(qd_ref.dtype)
        kd_ref[0, :, sl] = (kh * jnp.exp(glast - gcol)).astype(kd_ref.dtype)
        qk_ref[0, h] = (_dot_nt(qh.astype(BF16), khb) * decay).astype(qk_ref.dtype)
        egl_ref[0, 0, h:h + 1, :] = jnp.broadcast_to(jnp.exp(glast), (1, LANES))
        heads.append((sl, m, (v[:, sl] * beta).astype(BF16), (kb * eg).astype(BF16)))

    xs = _unit_lower_inverses([hd[1] for hd in heads], ts)
    for (sl, _, vb, kbe), x in zip(heads, xs):
        xb = x.astype(BF16)
        u_ref[0, :, sl] = _dot(xb, vb)
        w_ref[0, :, sl] = _dot(xb, kbe).astype(w_ref.dtype)


def _delta_scan_kernel(qd_ref, kd_ref, u_ref, w_ref, qk_ref, egl_ref, z_ref, g_ref, o_ref, st_ref):
    @pl.when(pl.program_id(1) == 0)
    def _():
        st_ref[...] = jnp.zeros_like(st_ref)

    for h in range(HEADS):
        sl = slice(h * HEAD_DIM, (h + 1) * HEAD_DIM)
        state = st_ref[h]
        sb = state.astype(BF16)
        v_new = u_ref[0, :, sl] - _dot(w_ref[0, :, sl], sb)
        vb = v_new.astype(BF16)
        o = _dot(qd_ref[0, :, sl], sb) + _dot(qk_ref[0, h], vb)
        st_ref[h] = state * egl_ref[0, 0, h:h + 1, 0:1] + _dot_tn(kd_ref[0, :, sl], vb)
        o = o * lax.rsqrt(jnp.mean(o * o, axis=-1, keepdims=True) + EPS) * g_ref[...]
        o_ref[0, :, sl] = (o * _silu(z_ref[0, :, sl])).astype(o_ref.dtype)


def _gated_deltanet(cd, col0, small, conv_w, a_log, dt_bias, norm_g, *, ts=256):
    bsz, s, _ = cd.shape
    w = MIX_W
    nc = s // ts
    hb = ts // CONV_HALO
    small_rows = jnp.swapaxes(small[:, :, :2 * HEADS], 1, 2)
    ea = jnp.exp(a_log.astype(F32))
    head_cols = jnp.zeros((2, LANES), F32)
    head_cols = head_cols.at[0, HEADS:2 * HEADS].set(ea).at[1, HEADS:2 * HEADS].set(dt_bias)
    head_rows = jnp.zeros((2 * HEADS, 2), F32)
    head_rows = head_rows.at[HEADS:, 0].set(ea).at[HEADS:, 1].set(dt_bias)

    def tile_spec(c):
        return pl.BlockSpec((1, ts, w), lambda b, i: (b, i, col0 + c))

    def halo_spec(c):
        return pl.BlockSpec((1, CONV_HALO, w), lambda b, i: (b, jnp.maximum(i * hb - 1, 0), col0 + c))

    const = lambda b, i: (0, 0)
    row_out = pl.BlockSpec((1, ts, w), lambda b, i: (b, i, 0))
    qd, kd, u, wm, qk, egl = pl.pallas_call(
        functools.partial(_delta_prep_kernel, ts=ts),
        grid=(bsz, nc),
        in_specs=[tile_spec(0), halo_spec(0), tile_spec(1), halo_spec(1), tile_spec(2), halo_spec(2),
                  pl.BlockSpec((1, ts, LANES), lambda b, i: (b, i, 0)),
                  pl.BlockSpec((1, 2 * HEADS, ts), lambda b, i: (b, 0, i)),
                  pl.BlockSpec((CONV_K, 3 * w), const),
                  pl.BlockSpec((2, LANES), const),
                  pl.BlockSpec((2 * HEADS, 2), const)],
        out_specs=[row_out, row_out, row_out, row_out,
                   pl.BlockSpec((1, HEADS, ts, ts), lambda b, i: (b, 0, i, 0)),
                   pl.BlockSpec((1, 1, 8, LANES), lambda b, i: (b, i, 0, 0))],
        out_shape=[jax.ShapeDtypeStruct((bsz, s, w), BF16),
                   jax.ShapeDtypeStruct((bsz, s, w), BF16),
                   jax.ShapeDtypeStruct((bsz, s, w), F32),
                   jax.ShapeDtypeStruct((bsz, s, w), BF16),
                   jax.ShapeDtypeStruct((bsz, HEADS, s, ts), BF16),
                   jax.ShapeDtypeStruct((bsz, nc, 8, LANES), F32)],
        scratch_shapes=[pltpu.VMEM((ts + CONV_HALO, w), F32)],
        compiler_params=_cparams("parallel", "parallel"),
        name="delta_prep",
    )(cd, cd, cd, cd, cd, cd, small, small_rows, conv_w, head_cols, head_rows)

    return pl.pallas_call(
        _delta_scan_kernel,
        grid=(bsz, nc),
        in_specs=[row_out, row_out, row_out, row_out,
                  pl.BlockSpec((1, HEADS, ts, ts), lambda b, i: (b, 0, i, 0)),
                  pl.BlockSpec((1, 1, 8, LANES), lambda b, i: (b, i, 0, 0)),
                  tile_spec(3),
                  pl.BlockSpec((1, HEAD_DIM), const)],
        out_specs=row_out,
        out_shape=jax.ShapeDtypeStruct((bsz, s, w), BF16),
        scratch_shapes=[pltpu.VMEM((HEADS, HEAD_DIM, HEAD_DIM), F32)],
        compiler_params=_cparams("parallel", "arbitrary"),
        name="delta_scan",
    )(qd, kd, u, wm, qk, egl, cd, norm_g)


def _top2_route(logits):
    lane = lax.broadcasted_iota(jnp.int32, logits.shape, 1)
    big = jnp.int32(LANES)
    v1 = jnp.max(logits, axis=-1, keepdims=True)
    i1 = jnp.min(jnp.where(logits == v1, lane, big), axis=-1, keepdims=True)
    rest = jnp.where(lane == i1, -jnp.inf, logits)
    v2 = jnp.max(rest, axis=-1, keepdims=True)
    i2 = jnp.min(jnp.where(rest == v2, lane, big), axis=-1, keepdims=True)
    e2 = jnp.exp(v2 - v1)
    p1 = 1.0 / (1.0 + e2)
    p2 = e2 / (1.0 + e2)
    out = jnp.where(lane == 0, i1.astype(F32), jnp.where(lane == 1, i2.astype(F32), 0.0))
    return jnp.where(lane == 2, p1, jnp.where(lane == 3, p2, out))


def _merge_kernel(oa_ref, ob_ref, oc_ref, od_ref, gate_ref, h_ref, wb_ref, wo_ref, g_ref, *rest, route):
    if route:
        wr_ref, h_out_ref, xn_ref, route_ref = rest
    else:
        h_out_ref, xn_ref = rest
    d = h_ref.shape[-1]
    y = None
    for n, o_ref in enumerate((oa_ref, ob_ref, oc_ref, od_ref)):
        yb = _dot(o_ref[...], wb_ref[n]) * gate_ref[:, n * d:(n + 1) * d].astype(F32)
        y = yb if y is None else y + yb
    h = h_ref[...] + _dot(y.astype(BF16), wo_ref[...])
    h_out_ref[...] = h
    xn = h * lax.rsqrt(jnp.mean(h * h, axis=-1, keepdims=True) + EPS) * g_ref[...]
    xn_ref[...] = xn.astype(xn_ref.dtype)
    if route:
        lane = lax.broadcasted_iota(jnp.int32, route_ref.shape, 1)
        logits = jnp.where(lane < N_EXPERTS, _dot_3pass(xn, wr_ref[...]), -jnp.inf)
        route_ref[...] = _top2_route(logits)


def _merge(branches, gates, h, wb, wo, g, wr, *, tm=512):
    t, d = h.shape
    w = MIX_W
    route = wr is not None
    row = lambda i: (i, 0)
    const2 = lambda i: (0, 0)
    in_specs = [pl.BlockSpec((tm, w), row)] * 4 + [
        pl.BlockSpec((tm, 4 * d), row),
        pl.BlockSpec((tm, d), row),
        pl.BlockSpec((4, w, d), lambda i: (0, 0, 0)),
        pl.BlockSpec((d, d), const2),
        pl.BlockSpec((1, d), const2)]
    out_specs = [pl.BlockSpec((tm, d), row), pl.BlockSpec((tm, d), row)]
    out_shape = [jax.ShapeDtypeStruct((t, d), F32), jax.ShapeDtypeStruct((t, d), F32 if route else BF16)]
    args = [*branches, gates, h, wb, wo, g]
    if route:
        in_specs.append(pl.BlockSpec((d, LANES), const2))
        out_specs.append(pl.BlockSpec((tm, LANES), row))
        out_shape.append(jax.ShapeDtypeStruct((t, LANES), F32))
        args.append(wr)
    return pl.pallas_call(
        functools.partial(_merge_kernel, route=route),
        grid=(t // tm,),
        in_specs=in_specs, out_specs=out_specs, out_shape=out_shape,
        compiler_params=_cparams("parallel"),
        name="merge",
    )(*args)


def _rms(h, g):
    return h * lax.rsqrt(jnp.mean(h * h, axis=-1, keepdims=True) + EPS) * g


def _ffn_kernel(x_ref, h_ref, w1_ref, w3_ref, w2_ref, gf_ref, o_ref, acc_ref, *, final_norm):
    f = pl.program_id(1)

    @pl.when(f == 0)
    def _():
        acc_ref[...] = h_ref[...]

    x = x_ref[...]
    a = _silu(_dot(x, w1_ref[...])) * _dot(x, w3_ref[...])
    acc_ref[...] += _dot(a.astype(BF16), w2_ref[...])

    @pl.when(f == pl.num_programs(1) - 1)
    def _():
        h = acc_ref[...]
        o_ref[...] = _rms(h, gf_ref[...]) if final_norm else h


def _ffn(xn, h, w1, w3, w2, g_final, *, final_norm, tm, tf):
    t, d = h.shape
    ff = w1.shape[1]
    row = lambda i, f: (i, 0)
    return pl.pallas_call(
        functools.partial(_ffn_kernel, final_norm=final_norm),
        grid=(t // tm, ff // tf),
        in_specs=[pl.BlockSpec((tm, d), row),
                  pl.BlockSpec((tm, d), row),
                  pl.BlockSpec((d, tf), lambda i, f: (0, f)),
                  pl.BlockSpec((d, tf), lambda i, f: (0, f)),
                  pl.BlockSpec((tf, d), lambda i, f: (f, 0)),
                  pl.BlockSpec((1, d), lambda i, f: (0, 0))],
        out_specs=pl.BlockSpec((tm, d), row),
        out_shape=jax.ShapeDtypeStruct((t, d), F32),
        scratch_shapes=[pltpu.VMEM((tm, d), F32)],
        compiler_params=_cparams("parallel", "arbitrary"),
        name="ffn",
    )(xn, h, w1, w3, w2, g_final)


def _row_copy(src_hbm, dst_ref, src_row, dst_row, sem):
    return pltpu.make_async_copy(src_hbm.at[pl.ds(src_row, 1)], dst_ref.at[pl.ds(dst_row, 1)], sem)


def _start_row_gather(idx_ref, idx_base, idx_stride, src_hbm, dst_ref, sem, n_rows):
    def issue(r, c):
        _row_copy(src_hbm, dst_ref, idx_ref[idx_base + r * idx_stride], r, sem).start()
        return c

    lax.fori_loop(0, n_rows, issue, 0, unroll=8)


def _wait_row_gather(src_hbm, dst_ref, sem, n_rows):
    def wait(r, c):
        _row_copy(src_hbm, dst_ref, 0, r, sem).wait()
        return c

    lax.fori_loop(0, n_rows, wait, 0, unroll=8)


def _gather_kernel(idx_ref, src_hbm, o_ref, sem, *, tg):
    _start_row_gather(idx_ref, pl.program_id(0) * tg, 1, src_hbm, o_ref, sem, tg)
    _wait_row_gather(src_hbm, o_ref, sem, tg)


def _gather_rows(src, idx, *, tg):
    n = idx.shape[0]
    d = src.shape[1]
    return pl.pallas_call(
        functools.partial(_gather_kernel, tg=tg),
        grid_spec=pltpu.PrefetchScalarGridSpec(
            num_scalar_prefetch=1,
            grid=(n // tg,),
            in_specs=[pl.BlockSpec(memory_space=pl.ANY)],
            out_specs=pl.BlockSpec((tg, d), lambda i, idx: (i, 0)),
            scratch_shapes=[pltpu.SemaphoreType.DMA(())]),
        out_shape=jax.ShapeDtypeStruct((n, d), src.dtype),
        compiler_params=_cparams("arbitrary"),
        name="gather_rows",
    )(idx, src)


def _expert_ffn_kernel(te_ref, nu_ref, x_ref, w1_ref, w3_ref, w2_ref, o_ref, xb_ref, acc_ref):
    i = pl.program_id(0)
    f = pl.program_id(1)
    used = i < nu_ref[0]

    @pl.when(used & (f == 0))
    def _():
        xb_ref[...] = x_ref[...].astype(BF16)

    @pl.when(used)
    def _():
        x = xb_ref[...]
        a = _silu(_dot(x, w1_ref[0])) * _dot(x, w3_ref[0])
        y = _dot(a.astype(BF16), w2_ref[0])

        @pl.when(f == 0)
        def _():
            acc_ref[...] = y

        @pl.when(f > 0)
        def _():
            acc_ref[...] += y

    last = f == pl.num_programs(1) - 1

    @pl.when(used & last)
    def _():
        o_ref[...] = acc_ref[...]

    @pl.when(jnp.logical_not(used) & last)
    def _():
        o_ref[...] = jnp.zeros_like(o_ref)


def _expert_ffn(x_sorted, tile_expert, n_used, w1, w3, w2, *, tm, tf):
    tp, d = x_sorted.shape
    ff = w1.shape[2]
    return pl.pallas_call(
        _expert_ffn_kernel,
        grid_spec=pltpu.PrefetchScalarGridSpec(
            num_scalar_prefetch=2,
            grid=(tp // tm, ff // tf),
            in_specs=[pl.BlockSpec((tm, d), lambda i, f, te, nu: (i, 0)),
                      pl.BlockSpec((1, d, tf), lambda i, f, te, nu: (te[i], 0, f)),
                      pl.BlockSpec((1, d, tf), lambda i, f, te, nu: (te[i], 0, f)),
                      pl.BlockSpec((1, tf, d), lambda i, f, te, nu: (te[i], f, 0))],
            out_specs=pl.BlockSpec((tm, d), lambda i, f, te, nu: (i, 0)),
            scratch_shapes=[pltpu.VMEM((tm, d), BF16), pltpu.VMEM((tm, d), F32)]),
        out_shape=jax.ShapeDtypeStruct((tp, d), F32),
        compiler_params=_cparams("parallel", "arbitrary"),
        name="expert_ffn",
    )(tile_expert, n_used, x_sorted, w1, w3, w2)


def _combine_kernel(pos_ref, y_hbm, h_ref, route_ref, gf_ref, o_ref, y1_ref, y2_ref, sem, *, tc, final_norm):
    base = pl.program_id(0) * tc * 2
    _start_row_gather(pos_ref, base, 2, y_hbm, y1_ref, sem.at[0], tc)
    _start_row_gather(pos_ref, base + 1, 2, y_hbm, y2_ref, sem.at[1], tc)
    _wait_row_gather(y_hbm, y1_ref, sem.at[0], tc)
    _wait_row_gather(y_hbm, y2_ref, sem.at[1], tc)
    route = route_ref[...]
    h = h_ref[...] + route[:, 2:3] * y1_ref[...] + route[:, 3:4] * y2_ref[...]
    o_ref[...] = _rms(h, gf_ref[...]) if final_norm else h


def _combine(y_sorted, pos, h, route, g_final, *, final_norm, tc):
    t, d = h.shape
    return pl.pallas_call(
        functools.partial(_combine_kernel, tc=tc, final_norm=final_norm),
        grid_spec=pltpu.PrefetchScalarGridSpec(
            num_scalar_prefetch=1,
            grid=(t // tc,),
            in_specs=[pl.BlockSpec(memory_space=pl.ANY),
                      pl.BlockSpec((tc, d), lambda i, pos: (i, 0)),
                      pl.BlockSpec((tc, LANES), lambda i, pos: (i, 0)),
                      pl.BlockSpec((1, d), lambda i, pos: (0, 0))],
            out_specs=pl.BlockSpec((tc, d), lambda i, pos: (i, 0)),
            scratch_shapes=[pltpu.VMEM((tc, d), F32), pltpu.VMEM((tc, d), F32),
                            pltpu.SemaphoreType.DMA((2,))]),
        out_shape=jax.ShapeDtypeStruct((t, d), F32),
        compiler_params=_cparams("arbitrary"),
        name="combine",
    )(pos, y_sorted, h, route, g_final)


def _dispatch_plan(route, tm):
    t = route.shape[0]
    e_flat = route[:, :2].astype(jnp.int32).reshape(-1)
    onehot = (e_flat[:, None] == jnp.arange(N_EXPERTS, dtype=jnp.int32)[None, :]).astype(jnp.int32)
    csum = jnp.cumsum(onehot, axis=0)
    rank = jnp.sum((csum - onehot) * onehot, axis=1)
    counts = csum[-1]
    padded = ((counts + tm - 1) // tm) * tm
    ends = jnp.cumsum(padded)
    pos = ((ends - padded)[e_flat] + rank).astype(jnp.int32)
    n_rows = 2 * t + N_EXPERTS * tm
    src_token = jnp.zeros((n_rows,), jnp.int32).at[pos].set(jnp.arange(2 * t, dtype=jnp.int32) // 2)
    tile_start = jnp.arange(n_rows // tm, dtype=jnp.int32) * tm
    tile_expert = jnp.minimum(jnp.searchsorted(ends, tile_start, side="right"), N_EXPERTS - 1)
    n_used = (ends[-1:] // tm).astype(jnp.int32)
    return src_token, pos, tile_expert.astype(jnp.int32), n_used


def _moe(xn, h, route, w1, w3, w2, g_final, *, final_norm, tm=1024, tf=512):
    src_token, pos, tile_expert, n_used = _dispatch_plan(route, tm)
    x_sorted = _gather_rows(xn, src_token, tg=2 * tm)
    y_sorted = _expert_ffn(x_sorted, tile_expert, n_used, w1, w3, w2, tm=tm, tf=tf)
    return _combine(y_sorted, pos, h, route, g_final, final_norm=final_norm, tc=1024)


def _block_diag(w):
    g, di, dj = w.shape
    eye = jnp.eye(g, dtype=w.dtype)
    return (eye[:, None, :, None] * w[:, :, None, :]).reshape(g * di, g * dj)


def kernel(x, norm_mix, norm_ffn, norm_final, w_in, lam_q1, lam_k1, lam_q2, lam_k2, a_subln, c_conv_w, c_conv_b, c_w_a, c_b_a, c_w_x, c_b_x, c_lambda, d_conv_w, d_A_log, d_dt_bias, d_norm, w_gate, b_gate, w_branch, w_out, ffn_w1, ffn_w3, ffn_w2, router, moe_w1, moe_w3, moe_w2):
    bsz, s, d = x.shape
    t = bsz * s
    depth = w_in.shape[0]
    n_ab = 6 * MIX_W
    n_cd = 6 * MIX_W
    h = x.reshape(t, d)
    for l in range(depth):
        g_mix = norm_mix[l].reshape(1, d)
        w_l = w_in[l]
        w_small = jnp.pad(w_l[:, n_ab + n_cd:], ((0, 0), (0, LANES - 2 * HEADS)))
        ab, cd, gates, small = _in_proj(
            h, g_mix, w_l[:, :n_ab].astype(BF16), w_l[:, n_ab:n_ab + n_cd].astype(BF16),
            w_gate[l].reshape(d, 4 * d).astype(BF16), b_gate[l].reshape(1, 4 * d), w_small)

        ab3 = ab.reshape(bsz, s, n_ab)
        cd3 = cd.reshape(bsz, s, n_cd)
        lam_vecs = jnp.stack([lam_q1[l], lam_k1[l], lam_q2[l], lam_k2[l]])
        o_a = _diff_attention(ab3, 0, lam_vecs, a_subln[l].reshape(1, HEAD_DIM), l)
        o_b = _moba_attention(ab3, 3)
        wbd = jnp.concatenate([_block_diag(c_w_a[l]), _block_diag(c_w_x[l])], axis=1).astype(BF16)
        bbd = jnp.concatenate([c_b_a[l].reshape(1, MIX_W), c_b_x[l].reshape(1, MIX_W)], axis=1)
        o_c = _rglru(cd3, 0, c_conv_w[l], c_conv_b[l].reshape(1, MIX_W), wbd, bbd,
                     c_lambda[l].reshape(1, MIX_W))
        o_d = _gated_deltanet(cd3, 2, small.reshape(bsz, s, LANES), d_conv_w[l], d_A_log[l],
                              d_dt_bias[l], d_norm[l].reshape(1, HEAD_DIM))

        moe_layer = l % 2 == 1
        wr = jnp.pad(router[l // 2], ((0, 0), (0, LANES - N_EXPERTS))) if moe_layer else None
        branches = [o.reshape(t, MIX_W) for o in (o_a, o_b, o_c, o_d)]
        merged = _merge(branches, gates, h, w_branch[l].astype(BF16), w_out[l].astype(BF16),
                        norm_ffn[l].reshape(1, d), wr)
        last = l == depth - 1
        g_final = norm_final.reshape(1, d)
        if moe_layer:
            h, xn, route = merged
            h = _moe(xn, h, route, moe_w1[l // 2].astype(BF16), moe_w3[l // 2].astype(BF16),
                     moe_w2[l // 2].astype(BF16), g_final, final_norm=last)
        else:
            h, xn = merged
            h = _ffn(xn, h, ffn_w1[l // 2].astype(BF16), ffn_w3[l // 2].astype(BF16),
                     ffn_w2[l // 2].astype(BF16), g_final, final_norm=last, tm=512, tf=1408)
    return h.reshape(bsz, s, d)
```
